```python
import math
import jax, jax.numpy as jnp
from jax import lax
import numpy as np

D_MODEL = 2048
BATCH = 4
SEQ = 4096
DEPTH = 2

N_EVEN = (DEPTH + 1) // 2
N_ODD = DEPTH // 2

CONV_CH = 1024
CONV_W = 31
DA_HEADS = 8
DA_HEAD_DIM = 64
DA_V_DIM = 2 * DA_HEAD_DIM
DA_QK = DA_HEADS * 2 * DA_HEAD_DIM
DA_WIDTH = DA_HEADS * DA_V_DIM
ROPE_THETA = 10000.0
Q_BLOCK = 128
SPLIT0 = [CONV_CH, CONV_CH, CONV_CH, DA_QK, DA_QK, DA_WIDTH, DA_WIDTH]
IN0 = sum(SPLIT0)
MIX0 = CONV_CH + DA_WIDTH
S5_WIDTH = D_MODEL
S5_GROUP = 16
S5_GROUPS = S5_WIDTH // S5_GROUP
S5_STATE = 64
S5_CHUNK = 128
EPS = 1e-6

kernel_name = "hybrid_conv_diffattn_s5_gated"


def rms_norm(x, g):
    xf = x.astype(jnp.float32)
    y = xf * lax.rsqrt(jnp.mean(xf * xf, axis=-1, keepdims=True) + EPS)
    return (y * g.astype(jnp.float32)).astype(x.dtype)


def layer_norm(x, g, b):
    xf = x.astype(jnp.float32)
    xc = xf - jnp.mean(xf, axis=-1, keepdims=True)
    var = jnp.mean(xc * xc, axis=-1, keepdims=True)
    return (xc * lax.rsqrt(var + EPS) * g.astype(jnp.float32) + b.astype(jnp.float32)).astype(x.dtype)


def rotary(x, pos):
    half = x.shape[-1] // 2
    freqs = ROPE_THETA ** (-jnp.arange(half, dtype=jnp.float32) / half)
    ang = pos.astype(jnp.float32)[:, None] * freqs[None, :]
    cos = jnp.cos(ang)[None, :, None, None, :]
    sin = jnp.sin(ang)[None, :, None, None, :]
    xf = x.astype(jnp.float32)
    x1, x2 = xf[..., :half], xf[..., half:]
    return jnp.concatenate([x1 * cos - x2 * sin, x2 * cos + x1 * sin], axis=-1)


def diff_attention(q, k, v, lam):
    b_, h_, _, s_, dh = q.shape
    nb = s_ // Q_BLOCK
    scale = dh ** -0.5
    qb = q.reshape(b_, h_, 2, nb, Q_BLOCK, dh).transpose(3, 0, 1, 2, 4, 5)
    k_pos = jnp.arange(s_)

    def block(args):
        qblk, i = args
        s = jnp.einsum('bhcqd,bhckd->bhcqk', qblk, k) * scale
        q_pos = i * Q_BLOCK + jnp.arange(Q_BLOCK)
        mask = k_pos[None, :] <= q_pos[:, None]
        s = jnp.where(mask, s, -jnp.inf)
        p = jax.nn.softmax(s, axis=-1)
        w = p[:, :, 0] - lam * p[:, :, 1]
        return jnp.einsum('bhqk,bhkd->bhqd', w, v)

    o = lax.map(block, (qb, jnp.arange(nb)))
    return o.transpose(1, 0, 3, 2, 4).reshape(b_, s_, h_, 2 * dh)


def even_layer(x, layer_idx, norm_g, w_in, conv_w, conv_b, cln_g, cln_b, qn_g, kn_g,
               lam_q1, lam_k1, lam_q2, lam_k2, subln_g, w_out):
    b_, s_, _ = x.shape
    pos = jnp.arange(s_)
    h = rms_norm(x, norm_g)
    proj = h @ w_in
    idx = [int(v) for v in np.cumsum(SPLIT0)[:-1]]
    a_val, a_glu, a_gate, q, k, v, b_gate = jnp.split(proj, idx, axis=-1)

    u = a_val * jax.nn.sigmoid(a_glu)
    kern = conv_w[:, None, :].astype(u.dtype)
    c = lax.conv_general_dilated(u, kern, window_strides=(1,), padding=[(CONV_W - 1, 0)],
                                 dimension_numbers=('NWC', 'WIO', 'NWC'),
                                 feature_group_count=CONV_CH) + conv_b
    c = jax.nn.silu(layer_norm(c, cln_g, cln_b))
    out_a = c * jax.nn.silu(a_gate)

    q = rotary(rms_norm(q.reshape(b_, s_, DA_HEADS, 2, DA_HEAD_DIM), qn_g), pos)
    k = rotary(rms_norm(k.reshape(b_, s_, DA_HEADS, 2, DA_HEAD_DIM), kn_g), pos)
    q = q.transpose(0, 2, 3, 1, 4)
    k = k.transpose(0, 2, 3, 1, 4)
    v = v.reshape(b_, s_, DA_HEADS, DA_V_DIM).astype(jnp.float32).transpose(0, 2, 1, 3)
    lam_init = 0.8 - 0.6 * math.exp(-0.3 * layer_idx)
    f32 = jnp.float32
    lam = (jnp.exp(jnp.sum(lam_q1.astype(f32) * lam_k1.astype(f32)))
           - jnp.exp(jnp.sum(lam_q2.astype(f32) * lam_k2.astype(f32))) + lam_init)
    o = diff_attention(q, k, v, lam)
    o = rms_norm(o, subln_g) * (1.0 - lam_init)
    out_b = o.reshape(b_, s_, DA_WIDTH).astype(x.dtype) * jax.nn.silu(b_gate)

    y = jnp.concatenate([out_a, out_b], axis=-1) @ w_out
    return x + y.astype(x.dtype)


def s5_scan(u, a_re, a_im, log_dt, b_re, b_im, c_re, c_im):
    f32 = jnp.float32
    a_re, a_im = a_re.astype(f32), a_im.astype(f32)
    b_re, b_im = b_re.astype(f32), b_im.astype(f32)
    c_re, c_im = c_re.astype(f32), c_im.astype(f32)
    b_, s_, g_, m_ = u.shape
    p_ = a_re.shape[-1]
    dt = jnp.exp(log_dt.astype(f32))[:, None]
    mag = jnp.exp(a_re * dt)
    lb_re, lb_im = mag * jnp.cos(a_im * dt), mag * jnp.sin(a_im * dt)
    den = a_re * a_re + a_im * a_im
    nr, ni = lb_re - 1.0, lb_im
    fr = (nr * a_re + ni * a_im) / den
    fi = (ni * a_re - nr * a_im) / den
    bb_re = fr[..., None] * b_re - fi[..., None] * b_im
    bb_im = fr[..., None] * b_im + fi[..., None] * b_re

    nc = s_ // S5_CHUNK
    uc = u.reshape(b_, nc, S5_CHUNK, g_, m_).transpose(1, 0, 2, 3, 4)
    ar = jnp.broadcast_to(lb_re[None, None], (1, S5_CHUNK, g_, p_))
    ai = jnp.broadcast_to(lb_im[None, None], (1, S5_CHUNK, g_, p_))

    def combine(e1, e2):
        ar1, ai1, br1, bi1 = e1
        ar2, ai2, br2, bi2 = e2
        return (ar2 * ar1 - ai2 * ai1, ar2 * ai1 + ai2 * ar1,
                ar2 * br1 - ai2 * bi1 + br2, ar2 * bi1 + ai2 * br1 + bi2)

    def step(carry, u_blk):
        h_re0, h_im0 = carry
        bu_re = jnp.einsum('blgm,gpm->blgp', u_blk, bb_re)
        bu_im = jnp.einsum('blgm,gpm->blgp', u_blk, bb_im)
        pr, pi, sr, si = lax.associative_scan(combine, (ar, ai, bu_re, bu_im), axis=1)
        h_re = sr + pr * h_re0[:, None] - pi * h_im0[:, None]
        h_im = si + pr * h_im0[:, None] + pi * h_re0[:, None]
        y = (jnp.einsum('blgp,gmp->blgm', h_re, c_re)
             - jnp.einsum('blgp,gmp->blgm', h_im, c_im))
        return (h_re[:, -1], h_im[:, -1]), y

    init = (jnp.zeros((b_, g_, p_), f32), jnp.zeros((b_, g_, p_), f32))
    _, ys = lax.scan(step, init, uc)
    return ys.transpose(1, 0, 2, 3, 4).reshape(b_, s_, g_, m_)


def odd_layer(x, norm_g, w_in, a_re, a_im, log_dt, b_re, b_im, c_re, c_im, d_skip,
              w_glu, b_glu, w_out):
    b_, s_, _ = x.shape
    h = rms_norm(x, norm_g)
    u, gate = jnp.split(h @ w_in, [S5_WIDTH], axis=-1)
    uf = u.astype(jnp.float32)
    y = s5_scan(uf.reshape(b_, s_, S5_GROUPS, S5_GROUP), a_re, a_im, log_dt,
                b_re, b_im, c_re, c_im).reshape(b_, s_, S5_WIDTH)
    y = y + d_skip.astype(jnp.float32) * uf
    z = jax.nn.gelu(y)
    z = z * jax.nn.sigmoid(z @ w_glu.astype(jnp.float32) + b_glu.astype(jnp.float32))
    out = z.astype(x.dtype) * jax.nn.silu(gate)
    return x + (out @ w_out).astype(x.dtype)


def setup_inputs(seed: int = 0) -> dict:
    key = jax.random.key(seed)
    ks = jax.random.split(key, 32)
    f32 = jnp.float32

    def nrm(k, shape, scale):
        return jax.random.normal(k, shape, f32) * scale

    ne, no = N_EVEN, N_ODD
    G, P, M, E = S5_GROUPS, S5_STATE, S5_GROUP, S5_WIDTH
    a_im_base = jnp.broadcast_to(math.pi * jnp.arange(P, dtype=f32), (no, G, P))
    return {
        "x": nrm(ks[0], (BATCH, SEQ, D_MODEL), 1.0),
        "e_norm_g": 1.0 + nrm(ks[1], (ne, D_MODEL), 0.02),
        "e_w_in": nrm(ks[2], (ne, D_MODEL, IN0), D_MODEL ** -0.5),
        "e_conv_w": nrm(ks[3], (ne, CONV_W, CONV_CH), CONV_W ** -0.5),
        "e_conv_b": nrm(ks[4], (ne, CONV_CH), 0.02),
        "e_cln_g": 1.0 + nrm(ks[5], (ne, CONV_CH), 0.02),
        "e_cln_b": nrm(ks[6], (ne, CONV_CH), 0.02),
        "e_qn_g": 1.0 + nrm(ks[7], (ne, DA_HEAD_DIM), 0.02),
        "e_kn_g": 1.0 + nrm(ks[8], (ne, DA_HEAD_DIM), 0.02),
        "e_lam_q1": nrm(ks[9], (ne, DA_HEAD_DIM), 0.1),
        "e_lam_k1": nrm(ks[10], (ne, DA_HEAD_DIM), 0.1),
        "e_lam_q2": nrm(ks[11], (ne, DA_HEAD_DIM), 0.1),
        "e_lam_k2": nrm(ks[12], (ne, DA_HEAD_DIM), 0.1),
        "e_subln_g": 1.0 + nrm(ks[13], (ne, DA_V_DIM), 0.02),
        "e_w_out": nrm(ks[14], (ne, MIX0, D_MODEL), MIX0 ** -0.5),
        "o_norm_g": 1.0 + nrm(ks[15], (no, D_MODEL), 0.02),
        "o_w_in": nrm(ks[16], (no, D_MODEL, 2 * E), D_MODEL ** -0.5),
        "o_A_re": -0.5 + nrm(ks[17], (no, G, P), 0.01),
        "o_A_im": a_im_base + nrm(ks[18], (no, G, P), 0.01),
        "o_log_dt": jax.random.uniform(ks[19], (no, G), f32, math.log(1e-3), math.log(1e-1)),
        "o_B_re": nrm(ks[20], (no, G, P, M), (2 * M) ** -0.5),
        "o_B_im": nrm(ks[21], (no, G, P, M), (2 * M) ** -0.5),
        "o_C_re": nrm(ks[22], (no, G, M, P), P ** -0.5),
        "o_C_im": nrm(ks[23], (no, G, M, P), P ** -0.5),
        "o_D": nrm(ks[24], (no, E), 1.0),
        "o_w_glu": nrm(ks[25], (no, E, E), E ** -0.5),
        "o_b_glu": nrm(ks[26], (no, E), 0.02),
        "o_w_out": nrm(ks[27], (no, E, D_MODEL), E ** -0.5),
    }


def reference(x, e_norm_g, e_w_in, e_conv_w, e_conv_b, e_cln_g, e_cln_b, e_qn_g, e_kn_g,
              e_lam_q1, e_lam_k1, e_lam_q2, e_lam_k2, e_subln_g, e_w_out,
              o_norm_g, o_w_in, o_A_re, o_A_im, o_log_dt, o_B_re, o_B_im, o_C_re, o_C_im,
              o_D, o_w_glu, o_b_glu, o_w_out):
    for layer in range(DEPTH):
        j = layer // 2
        if layer % 2 == 0:
            x = even_layer(x, layer, e_norm_g[j], e_w_in[j], e_conv_w[j], e_conv_b[j],
                           e_cln_g[j], e_cln_b[j], e_qn_g[j], e_kn_g[j],
                           e_lam_q1[j], e_lam_k1[j], e_lam_q2[j], e_lam_k2[j],
                           e_subln_g[j], e_w_out[j])
        else:
            x = odd_layer(x, o_norm_g[j], o_w_in[j], o_A_re[j], o_A_im[j], o_log_dt[j],
                          o_B_re[j], o_B_im[j], o_C_re[j], o_C_im[j], o_D[j],
                          o_w_glu[j], o_b_glu[j], o_w_out[j])
    return x
```

```python
import functools
import math

import jax
import jax.numpy as jnp
from jax import lax
from jax.experimental import pallas as pl
from jax.experimental.pallas import tpu as pltpu

F32 = jnp.float32
BF16 = jnp.bfloat16

EPS = 1e-6
ROPE_THETA = 10000.0
CONV_W = 31
DA_HEADS = 8
DA_HEAD_DIM = 64
S5_GROUP = 16
S5_STATE = 64

LANES = 128
CONV_HALO = 32
CONV_ROWS = 16
S5_CHUNK = 32
S5_GROUPS_PER_STEP = 4
VMEM_LIMIT = 56 * 1024 * 1024


def _params(*sem):
    return pltpu.CompilerParams(dimension_semantics=sem, vmem_limit_bytes=VMEM_LIMIT)


def _silu(x):
    return x * jax.nn.sigmoid(x)


def _rms_matmul_kernel(x_ref, g_ref, w_ref, o_ref, h_ref):
    @pl.when(pl.program_id(1) == 0)
    def _():
        x = x_ref[...]
        ms = jnp.mean(x * x, axis=-1, keepdims=True)
        h_ref[...] = (x * lax.rsqrt(ms + EPS) * g_ref[...]).astype(h_ref.dtype)

    o_ref[...] = jnp.dot(h_ref[...], w_ref[...], preferred_element_type=F32).astype(o_ref.dtype)


def _rms_matmul(x, g, w, *, tm, tn):
    m, d = x.shape
    n = w.shape[1]
    return pl.pallas_call(
        _rms_matmul_kernel,
        grid=(m // tm, n // tn),
        in_specs=[
            pl.BlockSpec((tm, d), lambda i, j: (i, 0)),
            pl.BlockSpec((1, d), lambda i, j: (0, 0)),
            pl.BlockSpec((d, tn), lambda i, j: (0, j)),
        ],
        out_specs=pl.BlockSpec((tm, tn), lambda i, j: (i, j)),
        out_shape=jax.ShapeDtypeStruct((m, n), F32),
        scratch_shapes=[pltpu.VMEM((tm, d), BF16)],
        compiler_params=_params("parallel", "arbitrary"),
        name="rms_matmul",
    )(x, g.reshape(1, d), w)


def _conv_kernel(val_ref, glu_ref, gate_ref, hval_ref, hglu_ref, w_ref, cb_ref, lg_ref, lb_ref,
                 o_ref, uext_ref, *, tt, tiles_per_seq):
    c = o_ref.shape[-1]
    first = (pl.program_id(0) % tiles_per_seq) == 0
    halo = hval_ref[...] * jax.nn.sigmoid(hglu_ref[...])
    uext_ref[0:CONV_HALO, :] = jnp.where(first, 0.0, halo)
    uext_ref[CONV_HALO:CONV_HALO + tt, :] = val_ref[...] * jax.nn.sigmoid(glu_ref[...])
    lg = lg_ref[...]
    lb = lb_ref[...]

    for base in range(0, tt, CONV_ROWS):
        acc = jnp.broadcast_to(cb_ref[...], (CONV_ROWS, c))
        for k in range(CONV_W):
            lo = base + CONV_HALO - CONV_W + 1 + k
            acc = acc + w_ref[k:k + 1, :] * uext_ref[lo:lo + CONV_ROWS, :]
        mean = jnp.mean(acc, axis=-1, keepdims=True)
        xc = acc - mean
        var = jnp.mean(xc * xc, axis=-1, keepdims=True)
        y = _silu(xc * lax.rsqrt(var + EPS) * lg + lb)
        o_ref[base:base + CONV_ROWS, :] = (y * _silu(gate_ref[base:base + CONV_ROWS, :])).astype(o_ref.dtype)


def _conv_mixer(proj, conv_w, conv_b, ln_g, ln_b, *, seq, tt):
    t = proj.shape[0]
    c = conv_w.shape[1]
    hb = tt // CONV_HALO
    row = lambda a: a.reshape(1, c)
    return pl.pallas_call(
        functools.partial(_conv_kernel, tt=tt, tiles_per_seq=seq // tt),
        grid=(t // tt,),
        in_specs=[
            pl.BlockSpec((tt, c), lambda i: (i, 0)),
            pl.BlockSpec((tt, c), lambda i: (i, 1)),
            pl.BlockSpec((tt, c), lambda i: (i, 2)),
            pl.BlockSpec((CONV_HALO, c), lambda i: (jnp.maximum(i * hb - 1, 0), 0)),
            pl.BlockSpec((CONV_HALO, c), lambda i: (jnp.maximum(i * hb - 1, 0), 1)),
            pl.BlockSpec((CONV_W, c), lambda i: (0, 0)),
            pl.BlockSpec((1, c), lambda i: (0, 0)),
            pl.BlockSpec((1, c), lambda i: (0, 0)),
            pl.BlockSpec((1, c), lambda i: (0, 0)),
        ],
        out_specs=pl.BlockSpec((tt, c), lambda i: (i, 0)),
        out_shape=jax.ShapeDtypeStruct((t, c), BF16),
        scratch_shapes=[pltpu.VMEM((CONV_HALO + tt, c), F32)],
        compiler_params=_params("parallel"),
        name="conv_mixer",
    )(proj, proj, proj, proj, proj, conv_w, row(conv_b), row(ln_g), row(ln_b))


def _qkv_kernel(q_ref, k_ref, v_ref, cos_ref, sin_ref, qg_ref, kg_ref, ones_ref,
                qo_ref, ko_ref, vo_ref, *, q_scale):
    cos = cos_ref[...]
    sin = sin_ref[...]
    ones = ones_ref[...]
    lane = lax.broadcasted_iota(jnp.int32, cos.shape, 1)
    first_half = (lane % DA_HEAD_DIM) < (DA_HEAD_DIM // 2)

    def prep(x, g, scale):
        x2 = x * x
        hi = x2.astype(BF16)
        lo = (x2 - hi.astype(F32)).astype(BF16)
        ss = (jnp.dot(hi, ones, preferred_element_type=F32)
              + jnp.dot(lo, ones, preferred_element_type=F32))
        y = x * lax.rsqrt(ss * (1.0 / DA_HEAD_DIM) + EPS) * g
        partner = jnp.where(first_half, pltpu.roll(y, LANES - DA_HEAD_DIM // 2, 1),
                            pltpu.roll(y, DA_HEAD_DIM // 2, 1))
        return (y * cos + partner * sin) * scale

    for h in range(q_ref.shape[-1] // LANES):
        sl = slice(h * LANES, (h + 1) * LANES)
        qo_ref[:, sl] = prep(q_ref[:, sl], qg_ref[...], q_scale).astype(qo_ref.dtype)
        ko_ref[:, sl] = prep(k_ref[:, sl], kg_ref[...], 1.0).astype(ko_ref.dtype)
    vo_ref[...] = v_ref[...].astype(vo_ref.dtype)


def _qkv_prep(proj, qn_g, kn_g, *, seq, tt, col0):
    t = proj.shape[0]
    w = DA_HEADS * 2 * DA_HEAD_DIM
    half = DA_HEAD_DIM // 2
    freqs = ROPE_THETA ** (-jnp.arange(half, dtype=F32) / half)
    ang = jnp.arange(seq, dtype=F32)[:, None] * freqs[None, :]
    cos = jnp.tile(jnp.cos(ang), (1, LANES // half))
    sin = jnp.tile(jnp.concatenate([-jnp.sin(ang), jnp.sin(ang)], axis=-1), (1, LANES // DA_HEAD_DIM))
    blk = jnp.arange(LANES) // DA_HEAD_DIM
    ones = (blk[:, None] == blk[None, :]).astype(BF16)
    tile_g = lambda g: jnp.tile(g.astype(F32), LANES // DA_HEAD_DIM).reshape(1, LANES)
    cb = col0 // w
    nseq = seq // tt
    out = jax.ShapeDtypeStruct((t, w), BF16)
    return pl.pallas_call(
        functools.partial(_qkv_kernel, q_scale=DA_HEAD_DIM ** -0.5),
        grid=(t // tt,),
        in_specs=[
            pl.BlockSpec((tt, w), lambda i: (i, cb)),
            pl.BlockSpec((tt, w), lambda i: (i, cb + 1)),
            pl.BlockSpec((tt, w), lambda i: (i, cb + 2)),
            pl.BlockSpec((tt, LANES), lambda i: (i % nseq, 0)),
            pl.BlockSpec((tt, LANES), lambda i: (i % nseq, 0)),
            pl.BlockSpec((1, LANES), lambda i: (0, 0)),
            pl.BlockSpec((1, LANES), lambda i: (0, 0)),
            pl.BlockSpec((LANES, LANES), lambda i: (0, 0)),
        ],
        out_specs=[pl.BlockSpec((tt, w), lambda i: (i, 0))] * 3,
        out_shape=[out, out, out],
        compiler_params=_params("parallel"),
        name="qkv_prep",
    )(proj, proj, proj, cos, sin, tile_g(qn_g), tile_g(kn_g), ones)


def _attn_kernel(lam_ref, q_ref, k_ref, v_ref, bg_ref, sg_ref, o_ref, *, tq, post_scale):
    i = pl.program_id(2)
    q = q_ref[...]
    lane = lax.broadcasted_iota(jnp.int32, q.shape, 1)
    zero = jnp.zeros_like(q)
    qs = jnp.concatenate([jnp.where(lane < DA_HEAD_DIM, q, zero),
                          jnp.where(lane >= DA_HEAD_DIM, q, zero)], axis=0)

    def step(j, carry, masked):
        m, l, acc = carry
        start = pl.multiple_of(j * tq, tq)
        k = k_ref[pl.ds(start, tq), :]
        v = v_ref[pl.ds(start, tq), :]
        s = lax.dot_general(qs, k, (((1,), (1,)), ((), ())), preferred_element_type=F32)
        if masked:
            row = lax.broadcasted_iota(jnp.int32, s.shape, 0) % tq
            col = lax.broadcasted_iota(jnp.int32, s.shape, 1)
            s = jnp.where(col <= row, s, -jnp.inf)
        m_new = jnp.maximum(m, jnp.max(s, axis=-1, keepdims=True))
        alpha = jnp.exp(m - m_new)
        p = jnp.exp(s - m_new)
        l = alpha * l + jnp.sum(p, axis=-1, keepdims=True)
        acc = alpha * acc + jnp.dot(p.astype(BF16), v, preferred_element_type=F32)
        return m_new, l, acc

    init = (jnp.full((2 * tq, 1), -jnp.inf, F32), jnp.zeros((2 * tq, 1), F32),
            jnp.zeros((2 * tq, LANES), F32))
    carry = lax.fori_loop(0, i, lambda j, c: step(j, c, False), init)
    _, l, acc = step(i, carry, True)
    o = acc / l
    d = o[:tq] - lam_ref[0] * o[tq:]
    ms = jnp.mean(d * d, axis=-1, keepdims=True)
    y = d * lax.rsqrt(ms + EPS) * sg_ref[...] * post_scale
    o_ref[...] = (y * _silu(bg_ref[...])).astype(o_ref.dtype)


def _diff_attention(lam, qn, kn, vb, proj, subln_g, *, batch, seq, tq, gate_col0, post_scale):
    t, w = qn.shape
    nq = seq // tq
    gcb = gate_col0 // LANES
    return pl.pallas_call(
        functools.partial(_attn_kernel, tq=tq, post_scale=post_scale),
        grid=(batch, DA_HEADS, nq),
        in_specs=[
            pl.BlockSpec(memory_space=pltpu.SMEM),
            pl.BlockSpec((tq, LANES), lambda b, h, i: (b * nq + i, h)),
            pl.BlockSpec((seq, LANES), lambda b, h, i: (b, h)),
            pl.BlockSpec((seq, LANES), lambda b, h, i: (b, h)),
            pl.BlockSpec((tq, LANES), lambda b, h, i: (b * nq + i, gcb + h)),
            pl.BlockSpec((1, LANES), lambda b, h, i: (0, 0)),
        ],
        out_specs=pl.BlockSpec((tq, LANES), lambda b, h, i: (b * nq + i, h)),
        out_shape=jax.ShapeDtypeStruct((t, w), BF16),
        compiler_params=_params("parallel", "parallel", "arbitrary"),
        name="diff_attention",
    )(lam.reshape(1), qn, kn, vb, proj, subln_g.astype(F32).reshape(1, LANES))


def _proj_res_kernel(a_ref, b_ref, wa_ref, wb_ref, x_ref, o_ref):
    y = (jnp.dot(a_ref[...], wa_ref[...], preferred_element_type=F32)
         + jnp.dot(b_ref[...], wb_ref[...], preferred_element_type=F32))
    o_ref[...] = x_ref[...] + y


def _proj_residual(a, a_cb, b, b_cb, w, x, *, tm, tn):
    m, n = x.shape
    kh = w.shape[0] // 2
    return pl.pallas_call(
        _proj_res_kernel,
        grid=(m // tm, n // tn),
        in_specs=[
            pl.BlockSpec((tm, kh), lambda i, j: (i, a_cb)),
            pl.BlockSpec((tm, kh), lambda i, j: (i, b_cb)),
            pl.BlockSpec((kh, tn), lambda i, j: (0, j)),
            pl.BlockSpec((kh, tn), lambda i, j: (1, j)),
            pl.BlockSpec((tm, tn), lambda i, j: (i, j)),
        ],
        out_specs=pl.BlockSpec((tm, tn), lambda i, j: (i, j)),
        out_shape=jax.ShapeDtypeStruct((m, n), F32),
        compiler_params=_params("parallel", "arbitrary"),
        name="proj_residual",
    )(a, b, w, w, x)


def _s5_kernel(ut_ref, toep_ref, sw_ref, cw_ref, lr_ref, li_ref, y_ref, *, chunks_per_seq):
    p = S5_STATE
    nl, gb, mm, ncol = ut_ref.shape
    reps = ncol // LANES
    lane = lax.broadcasted_iota(jnp.int32, (p, ncol), 1)
    cpos = lane % chunks_per_seq

    def shifted(x, k):
        return jnp.where(cpos >= k, pltpu.roll(x, k, 1), 0.0)

    for g in range(gb):
        u = ut_ref[:, g].reshape(nl * mm, ncol)
        s = jnp.dot(sw_ref[g], u, preferred_element_type=F32)
        xr, xi = s[:p], s[p:]
        lr = jnp.concatenate([lr_ref[g]] * reps, axis=1)
        li = jnp.concatenate([li_ref[g]] * reps, axis=1)
        k = 1
        while k < chunks_per_seq:
            sr, si = shifted(xr, k), shifted(xi, k)
            xr, xi = xr + (lr * sr - li * si), xi + (lr * si + li * sr)
            lr, li = lr * lr - li * li, 2.0 * (lr * li)
            k *= 2
        h = jnp.concatenate([shifted(xr, 1), shifted(xi, 1)], axis=0).astype(BF16)
        y = (jnp.dot(toep_ref[g], u, preferred_element_type=F32)
             + jnp.dot(cw_ref[g], h, preferred_element_type=F32))
        y_ref[:, g] = y.reshape(nl, mm, ncol)


def _s5_tables(a_re, a_im, log_dt, b_re, b_im, c_re, c_im, chunk):
    hp = lax.Precision.HIGHEST
    g, p, m = b_re.shape
    dt = jnp.exp(log_dt.astype(F32))[:, None]
    a_re, a_im = a_re.astype(F32), a_im.astype(F32)
    mag = jnp.exp(a_re * dt)
    lb_re, lb_im = mag * jnp.cos(a_im * dt), mag * jnp.sin(a_im * dt)
    den = a_re * a_re + a_im * a_im
    nr, ni = lb_re - 1.0, lb_im
    fr = (nr * a_re + ni * a_im) / den
    fi = (ni * a_re - nr * a_im) / den
    bb_re = fr[..., None] * b_re - fi[..., None] * b_im
    bb_im = fr[..., None] * b_im + fi[..., None] * b_re
    d = jnp.arange(chunk + 1, dtype=F32)[None, :, None]
    pmag = jnp.exp((a_re * dt)[:, None, :] * d)
    pang = (a_im * dt)[:, None, :] * d
    pr, pi = pmag * jnp.cos(pang), pmag * jnp.sin(pang)
    er = pr[..., None] * bb_re[:, None] - pi[..., None] * bb_im[:, None]
    ei = pr[..., None] * bb_im[:, None] + pi[..., None] * bb_re[:, None]
    c_re, c_im = c_re.astype(F32), c_im.astype(F32)
    kern = (jnp.einsum('gmp,gdpn->gdmn', c_re, er[:, :chunk], precision=hp)
            - jnp.einsum('gmp,gdpn->gdmn', c_im, ei[:, :chunk], precision=hp))
    l = jnp.arange(chunk)
    lag = l[:, None] - l[None, :]
    toep = jnp.where((lag >= 0)[None, :, :, None, None], kern[:, jnp.clip(lag, 0, chunk - 1)], 0.0)
    toep = toep.transpose(0, 1, 3, 2, 4).reshape(g, chunk * m, chunk * m)
    rev = chunk - 1 - l
    sw = jnp.concatenate([er[:, rev].transpose(0, 2, 1, 3), ei[:, rev].transpose(0, 2, 1, 3)], axis=1)
    sw = sw.reshape(g, 2 * p, chunk * m)
    pr1, pi1 = pr[:, 1:, None, :], pi[:, 1:, None, :]
    cr, ci = c_re[:, None], c_im[:, None]
    cw = jnp.concatenate([cr * pr1 - ci * pi1, -(cr * pi1 + ci * pr1)], axis=-1)
    cw = cw.reshape(g, chunk * m, 2 * p)
    lam_r = jnp.broadcast_to(pr[:, chunk, :, None], (g, p, LANES))
    lam_i = jnp.broadcast_to(pi[:, chunk, :, None], (g, p, LANES))
    return toep.astype(BF16), sw.astype(BF16), cw.astype(BF16), lam_r, lam_i


def _s5(u, tables, *, batch, seq):
    toep, sw, cw, lam_r, lam_i = tables
    t, e = u.shape
    g = toep.shape[0]
    m = e // g
    chunk = toep.shape[1] // m
    ncol = t // chunk
    gb = S5_GROUPS_PER_STEP
    ut = u.reshape(ncol, chunk * e).T.reshape(chunk, g, m, ncol)
    wspec = lambda a: pl.BlockSpec((gb,) + a.shape[1:], lambda i: (i, 0, 0))
    yt = pl.pallas_call(
        functools.partial(_s5_kernel, chunks_per_seq=seq // chunk),
        grid=(g // gb,),
        in_specs=[pl.BlockSpec((chunk, gb, m, ncol), lambda i: (0, i, 0, 0)),
                  wspec(toep), wspec(sw), wspec(cw), wspec(lam_r), wspec(lam_i)],
        out_specs=pl.BlockSpec((chunk, gb, m, ncol), lambda i: (0, i, 0, 0)),
        out_shape=jax.ShapeDtypeStruct((chunk, g, m, ncol), F32),
        compiler_params=_params("parallel"),
        name="s5_chunked",
    )(ut, toep, sw, cw, lam_r, lam_i)
    return yt.reshape(chunk * e, ncol).T.reshape(t, e)


def _glu_kernel(ys_ref, u_ref, gate_ref, d_ref, w_ref, b_ref, o_ref):
    y = ys_ref[...] + d_ref[...] * u_ref[...]
    z = 0.5 * y * (1.0 + jnp.tanh(math.sqrt(2.0 / math.pi) * (y + 0.044715 * (y * y * y))))
    t = jnp.dot(z.astype(BF16), w_ref[...], preferred_element_type=F32) + b_ref[...]
    o_ref[...] = (z * jax.nn.sigmoid(t) * _silu(gate_ref[...])).astype(o_ref.dtype)


def _glu(ys, proj, d_skip, w_glu, b_glu, *, tm):
    t, e = ys.shape
    return pl.pallas_call(
        _glu_kernel,
        grid=(t // tm,),
        in_specs=[
            pl.BlockSpec((tm, e), lambda i: (i, 0)),
            pl.BlockSpec((tm, e), lambda i: (i, 0)),
            pl.BlockSpec((tm, e), lambda i: (i, 1)),
            pl.BlockSpec((1, e), lambda i: (0, 0)),
            pl.BlockSpec((e, e), lambda i: (0, 0)),
            pl.BlockSpec((1, e), lambda i: (0, 0)),
        ],
        out_specs=pl.BlockSpec((tm, e), lambda i: (i, 0)),
        out_shape=jax.ShapeDtypeStruct((t, e), BF16),
        compiler_params=_params("parallel"),
        name="glu_gate",
    )(ys, proj, proj, d_skip.astype(F32).reshape(1, e), w_glu, b_glu.astype(F32).reshape(1, e))


def _even_layer(x, layer_idx, norm_g, w_in, conv_w, conv_b, cln_g, cln_b, qn_g, kn_g,
                lam_q1, lam_k1, lam_q2, lam_k2, subln_g, w_out, *, batch, seq):
    conv_ch = conv_w.shape[1]
    proj = _rms_matmul(x, norm_g, w_in.astype(BF16), tm=1024, tn=1024)
    mix_a = _conv_mixer(proj, conv_w.astype(F32), conv_b.astype(F32), cln_g.astype(F32),
                        cln_b.astype(F32), seq=seq, tt=128)
    qn, kn, vb = _qkv_prep(proj, qn_g, kn_g, seq=seq, tt=512, col0=3 * conv_ch)
    lam_init = 0.8 - 0.6 * math.exp(-0.3 * layer_idx)
    lam = (jnp.exp(jnp.sum(lam_q1.astype(F32) * lam_k1.astype(F32)))
           - jnp.exp(jnp.sum(lam_q2.astype(F32) * lam_k2.astype(F32))) + lam_init)
    qkv_w = qn.shape[1]
    mix_b = _diff_attention(lam, qn, kn, vb, proj, subln_g, batch=batch, seq=seq, tq=512,
                            gate_col0=3 * conv_ch + 3 * qkv_w, post_scale=1.0 - lam_init)
    return _proj_residual(mix_a, 0, mix_b, 0, w_out.astype(BF16), x, tm=1024, tn=1024)


def _odd_layer(x, norm_g, w_in, a_re, a_im, log_dt, b_re, b_im, c_re, c_im, d_skip,
               w_glu, b_glu, w_out, *, batch, seq):
    e = w_glu.shape[0]
    proj = _rms_matmul(x, norm_g, w_in.astype(BF16), tm=1024, tn=1024)
    tables = _s5_tables(a_re, a_im, log_dt, b_re, b_im, c_re, c_im, S5_CHUNK)
    ys = _s5(proj[:, :e].astype(BF16), tables, batch=batch, seq=seq)
    out = _glu(ys, proj, d_skip, w_glu.astype(BF16), b_glu, tm=256)
    return _proj_residual(out, 0, out, 1, w_out.astype(BF16), x, tm=1024, tn=1024)


def kernel(x, e_norm_g, e_w_in, e_conv_w, e_conv_b, e_cln_g, e_cln_b, e_qn_g, e_kn_g, e_lam_q1, e_lam_k1, e_lam_q2, e_lam_k2, e_subln_g, e_w_out, o_norm_g, o_w_in, o_A_re, o_A_im, o_log_dt, o_B_re, o_B_im, o_C_re, o_C_im, o_D, o_w_glu, o_b_glu, o_w_out):
    batch, seq, d_model = x.shape
    depth = e_norm_g.shape[0] + o_norm_g.shape[0]
    h = x.reshape(batch * seq, d_model)
    for layer in range(depth):
        j = layer // 2
        if layer % 2 == 0:
            h = _even_layer(h, layer, e_norm_g[j], e_w_in[j], e_conv_w[j], e_conv_b[j], e_cln_g[j],
                            e_cln_b[j], e_qn_g[j], e_kn_g[j], e_lam_q1[j], e_lam_k1[j], e_lam_q2[j],
                            e_lam_k2[j], e_subln_g[j], e_w_out[j], batch=batch, seq=seq)
        else:
            h = _odd_layer(h, o_norm_g[j], o_w_in[j], o_A_re[j], o_A_im[j], o_log_dt[j], o_B_re[j],
                           o_B_im[j], o_C_re[j], o_C_im[j], o_D[j], o_w_glu[j], o_b_glu[j], o_w_out[j],
                           batch=batch, seq=seq)
    return h.reshape(batch, seq, d_model)
```

```python
import functools
import math

import jax
import jax.numpy as jnp
from jax import lax
from jax.experimental import pallas as pl
from jax.experimental.pallas import tpu as pltpu

F32 = jnp.float32
BF16 = jnp.bfloat16

EPS = 1e-6
ROPE_THETA = 10000.0
CONV_W = 31
DA_HEADS = 8
DA_HEAD_DIM = 64
S5_GROUP = 16
S5_STATE = 64

LANES = 128
CONV_HALO = 32
CONV_ROWS = 16
ATT_ROWS = 64
S5_CHUNK = 32
S5_GROUPS_PER_STEP = 4
VMEM_LIMIT = 56 * 1024 * 1024


def _params(*sem):
    return pltpu.CompilerParams(dimension_semantics=sem, vmem_limit_bytes=VMEM_LIMIT)


def _silu(x):
    return x * jax.nn.sigmoid(x)


def _rms_matmul_kernel(x_ref, g_ref, w_ref, o_ref, h_ref):
    @pl.when(pl.program_id(1) == 0)
    def _():
        x = x_ref[...]
        ms = jnp.mean(x * x, axis=-1, keepdims=True)
        h_ref[...] = (x * lax.rsqrt(ms + EPS) * g_ref[...]).astype(h_ref.dtype)

    o_ref[...] = jnp.dot(h_ref[...], w_ref[...], preferred_element_type=F32).astype(o_ref.dtype)


def _rms_matmul(x, g, w, *, tm, tn):
    m, d = x.shape
    n = w.shape[1]
    return pl.pallas_call(
        _rms_matmul_kernel,
        grid=(m // tm, n // tn),
        in_specs=[
            pl.BlockSpec((tm, d), lambda i, j: (i, 0)),
            pl.BlockSpec((1, d), lambda i, j: (0, 0)),
            pl.BlockSpec((d, tn), lambda i, j: (0, j)),
        ],
        out_specs=pl.BlockSpec((tm, tn), lambda i, j: (i, j)),
        out_shape=jax.ShapeDtypeStruct((m, n), F32),
        scratch_shapes=[pltpu.VMEM((tm, d), BF16)],
        compiler_params=_params("parallel", "arbitrary"),
        name="rms_matmul",
    )(x, g.reshape(1, d), w)


def _conv_kernel(val_ref, glu_ref, gate_ref, hval_ref, hglu_ref, w_ref, cb_ref, lg_ref, lb_ref,
                 o_ref, uext_ref, *, tt, tiles_per_seq):
    c = o_ref.shape[-1]
    nlt = c // LANES
    first = (pl.program_id(0) % tiles_per_seq) == 0
    halo = jnp.where(first, 0.0, hval_ref[...] * jax.nn.sigmoid(hglu_ref[...]))
    cur = val_ref[...] * jax.nn.sigmoid(glu_ref[...])
    for j in range(nlt):
        sl = slice(j * LANES, (j + 1) * LANES)
        uext_ref[j, pl.ds(0, CONV_HALO, stride=2), :] = halo[:, sl]
        uext_ref[j, pl.ds(2 * CONV_HALO, tt, stride=2), :] = cur[:, sl]
    lg = lg_ref[...]
    lb = lb_ref[...]

    def chunk(r, carry):
        base = pl.multiple_of(r * CONV_ROWS, CONV_ROWS)
        accs = []
        for j in range(nlt):
            sl = slice(j * LANES, (j + 1) * LANES)
            acc = jnp.broadcast_to(cb_ref[:, sl], (CONV_ROWS, LANES))
            for k in range(CONV_W):
                row = base + (CONV_HALO - CONV_W + 1 + k)
                acc = acc + w_ref[k:k + 1, sl] * uext_ref[j, pl.ds(2 * row, CONV_ROWS, stride=2), :]
            accs.append(acc)
        acc = jnp.concatenate(accs, axis=1)
        mean = jnp.mean(acc, axis=-1, keepdims=True)
        xc = acc - mean
        var = jnp.mean(xc * xc, axis=-1, keepdims=True)
        y = _silu(xc * lax.rsqrt(var + EPS) * lg + lb)
        o_ref[pl.ds(base, CONV_ROWS), :] = (y * _silu(gate_ref[pl.ds(base, CONV_ROWS), :])).astype(o_ref.dtype)
        return carry

    lax.fori_loop(0, tt // CONV_ROWS, chunk, 0)


def _conv_mixer(proj, conv_w, conv_b, ln_g, ln_b, *, seq, tt):
    t = proj.shape[0]
    c = conv_w.shape[1]
    hb = tt // CONV_HALO
    row = lambda a: a.reshape(1, c)
    return pl.pallas_call(
        functools.partial(_conv_kernel, tt=tt, tiles_per_seq=seq // tt),
        grid=(t // tt,),
        in_specs=[
            pl.BlockSpec((tt, c), lambda i: (i, 0)),
            pl.BlockSpec((tt, c), lambda i: (i, 1)),
            pl.BlockSpec((tt, c), lambda i: (i, 2)),
            pl.BlockSpec((CONV_HALO, c), lambda i: (jnp.maximum(i * hb - 1, 0), 0)),
            pl.BlockSpec((CONV_HALO, c), lambda i: (jnp.maximum(i * hb - 1, 0), 1)),
            pl.BlockSpec((CONV_W, c), lambda i: (0, 0)),
            pl.BlockSpec((1, c), lambda i: (0, 0)),
            pl.BlockSpec((1, c), lambda i: (0, 0)),
            pl.BlockSpec((1, c), lambda i: (0, 0)),
        ],
        out_specs=pl.BlockSpec((tt, c), lambda i: (i, 0)),
        out_shape=jax.ShapeDtypeStruct((t, c), BF16),
        scratch_shapes=[pltpu.VMEM((c // LANES, 2 * (CONV_HALO + tt), LANES), F32)],
        compiler_params=_params("parallel"),
        name="conv_mixer",
    )(proj, proj, proj, proj, proj, conv_w, row(conv_b), row(ln_g), row(ln_b))


def _qkv_kernel(q_ref, k_ref, v_ref, cos_ref, sin_ref, qg_ref, kg_ref, ones_ref,
                qo_ref, ko_ref, vo_ref, *, q_scale):
    cos = cos_ref[...]
    sin = sin_ref[...]
    ones = ones_ref[...]
    lane = lax.broadcasted_iota(jnp.int32, cos.shape, 1)
    first_half = (lane % DA_HEAD_DIM) < (DA_HEAD_DIM // 2)

    def prep(x, g, scale):
        x2 = x * x
        hi = x2.astype(BF16)
        lo = (x2 - hi.astype(F32)).astype(BF16)
        ss = (jnp.dot(hi, ones, preferred_element_type=F32)
              + jnp.dot(lo, ones, preferred_element_type=F32))
        y = x * lax.rsqrt(ss * (1.0 / DA_HEAD_DIM) + EPS) * g
        partner = jnp.where(first_half, pltpu.roll(y, LANES - DA_HEAD_DIM // 2, 1),
                            pltpu.roll(y, DA_HEAD_DIM // 2, 1))
        return (y * cos + partner * sin) * scale

    for h in range(q_ref.shape[-1] // LANES):
        sl = slice(h * LANES, (h + 1) * LANES)
        qo_ref[:, sl] = prep(q_ref[:, sl], qg_ref[...], q_scale).astype(qo_ref.dtype)
        ko_ref[:, sl] = prep(k_ref[:, sl], kg_ref[...], 1.0).astype(ko_ref.dtype)
    vo_ref[...] = v_ref[...].astype(vo_ref.dtype)


def _qkv_prep(proj, qn_g, kn_g, *, seq, tt, col0):
    t = proj.shape[0]
    w = DA_HEADS * 2 * DA_HEAD_DIM
    half = DA_HEAD_DIM // 2
    freqs = ROPE_THETA ** (-jnp.arange(half, dtype=F32) / half)
    ang = jnp.arange(seq, dtype=F32)[:, None] * freqs[None, :]
    cos = jnp.tile(jnp.cos(ang), (1, LANES // half))
    sin = jnp.tile(jnp.concatenate([-jnp.sin(ang), jnp.sin(ang)], axis=-1), (1, LANES // DA_HEAD_DIM))
    blk = jnp.arange(LANES) // DA_HEAD_DIM
    ones = (blk[:, None] == blk[None, :]).astype(BF16)
    tile_g = lambda g: jnp.tile(g.astype(F32), LANES // DA_HEAD_DIM).reshape(1, LANES)
    cb = col0 // w
    nseq = seq // tt
    out = jax.ShapeDtypeStruct((t, w), BF16)
    return pl.pallas_call(
        functools.partial(_qkv_kernel, q_scale=math.log2(math.e) * DA_HEAD_DIM ** -0.5),
        grid=(t // tt,),
        in_specs=[
            pl.BlockSpec((tt, w), lambda i: (i, cb)),
            pl.BlockSpec((tt, w), lambda i: (i, cb + 1)),
            pl.BlockSpec((tt, w), lambda i: (i, cb + 2)),
            pl.BlockSpec((tt, LANES), lambda i: (i % nseq, 0)),
            pl.BlockSpec((tt, LANES), lambda i: (i % nseq, 0)),
            pl.BlockSpec((1, LANES), lambda i: (0, 0)),
            pl.BlockSpec((1, LANES), lambda i: (0, 0)),
            pl.BlockSpec((LANES, LANES), lambda i: (0, 0)),
        ],
        out_specs=[pl.BlockSpec((tt, w), lambda i: (i, 0))] * 3,
        out_shape=[out, out, out],
        compiler_params=_params("parallel"),
        name="qkv_prep",
    )(proj, proj, proj, cos, sin, tile_g(qn_g), tile_g(kn_g), ones)


def _attn_kernel(lam_ref, q_ref, k_ref, v_ref, bg_ref, sg_ref, o_ref,
                 qs_ref, vaug_ref, s0_ref, s1_ref, s2_ref, p0_ref, p1_ref, p2_ref, m_ref,
                 a0_ref, a1_ref, a2_ref, acc_ref, *, tq, post_scale):
    i = pl.program_id(2)
    s_refs, p_refs, a_refs = (s0_ref, s1_ref, s2_ref), (p0_ref, p1_ref, p2_ref), (a0_ref, a1_ref, a2_ref)

    @pl.when(i == 0)
    def _():
        vaug_ref[:, 0:LANES] = v_ref[...]
        vaug_ref[:, LANES:2 * LANES] = jnp.ones(v_ref.shape, vaug_ref.dtype)

    q = q_ref[...]
    lane = lax.broadcasted_iota(jnp.int32, q.shape, 1)
    zero = jnp.zeros_like(q)
    qs_ref[0:tq, :] = jnp.where(lane < DA_HEAD_DIM, q, zero)
    qs_ref[tq:2 * tq, :] = jnp.where(lane >= DA_HEAD_DIM, q, zero)
    m_ref[...] = jnp.full(m_ref.shape, -jnp.inf, F32)
    acc_ref[...] = jnp.zeros(acc_ref.shape, F32)
    reps = tq // LANES

    def scores(j, slot):
        start = pl.multiple_of(j * tq, tq)
        s_refs[slot][...] = lax.dot_general(qs_ref[...], k_ref[pl.ds(start, tq), :],
                                            (((1,), (1,)), ((), ())), preferred_element_type=F32)

    def softmax(slot, masked):
        s_ref, p_ref, a_ref = s_refs[slot], p_refs[slot], a_refs[slot]
        for r0 in range(0, 2 * tq, ATT_ROWS):
            rs = slice(r0, r0 + ATT_ROWS)
            s = s_ref[rs, :]
            if masked:
                row = (r0 % tq) + lax.broadcasted_iota(jnp.int32, s.shape, 0)
                col = lax.broadcasted_iota(jnp.int32, s.shape, 1)
                s = jnp.where(col <= row, s, -jnp.inf)
            m_prev = m_ref[rs, :]
            m_new = jnp.maximum(m_prev, jnp.max(s, axis=-1, keepdims=True))
            a_ref[rs, :] = jnp.exp2(m_prev - m_new)
            m_ref[rs, :] = m_new
            p_ref[rs, :] = jnp.exp2(s - jnp.concatenate([m_new] * reps, axis=1)).astype(p_ref.dtype)

    def accumulate(j, slot):
        start = pl.multiple_of(j * tq, tq)
        alpha = a_refs[slot][...]
        acc_ref[...] = jnp.concatenate([alpha, alpha], axis=1) * acc_ref[...] + jnp.dot(
            p_refs[slot][...], vaug_ref[pl.ds(start, tq), :], preferred_element_type=F32)

    scores(0, 0)

    @pl.when(i == 0)
    def _():
        softmax(0, True)
        accumulate(0, 0)

    @pl.when(i > 0)
    def _():
        scores(1, 1)
        softmax(0, False)
        npairs = (i - 1) // 2

        def pair(tt, carry):
            t = 2 * tt
            scores(t + 2, 0)
            softmax(1, False)
            accumulate(t, 0)
            scores(t + 3, 1)
            softmax(0, False)
            accumulate(t + 1, 1)
            return carry

        lax.fori_loop(0, npairs, pair, 0)

        @pl.when(i % 2 == 1)
        def _():
            softmax(1, True)
            accumulate(i - 1, 0)
            accumulate(i, 1)

        @pl.when(i % 2 == 0)
        def _():
            scores(i, 2)
            softmax(1, False)
            accumulate(i - 2, 0)
            softmax(2, True)
            accumulate(i - 1, 1)
            accumulate(i, 2)

    o = acc_ref[:, 0:LANES] / acc_ref[:, LANES:2 * LANES]
    d = o[:tq] - lam_ref[0] * o[tq:]
    ms = jnp.mean(d * d, axis=-1, keepdims=True)
    y = d * lax.rsqrt(ms + EPS) * sg_ref[...] * post_scale
    o_ref[...] = (y * _silu(bg_ref[...])).astype(o_ref.dtype)


def _diff_attention(lam, qn, kn, vb, proj, subln_g, *, batch, seq, tq, gate_col0, post_scale):
    t, w = qn.shape
    nq = seq // tq
    gcb = gate_col0 // LANES
    stat = pltpu.VMEM((2 * tq, LANES), F32)
    sbuf = pltpu.VMEM((2 * tq, tq), F32)
    pbuf = pltpu.VMEM((2 * tq, tq), BF16)
    return pl.pallas_call(
        functools.partial(_attn_kernel, tq=tq, post_scale=post_scale),
        grid=(batch, DA_HEADS, nq),
        in_specs=[
            pl.BlockSpec(memory_space=pltpu.SMEM),
            pl.BlockSpec((tq, LANES), lambda b, h, i: (b * nq + i, h)),
            pl.BlockSpec((seq, LANES), lambda b, h, i: (b, h)),
            pl.BlockSpec((seq, LANES), lambda b, h, i: (b, h)),
            pl.BlockSpec((tq, LANES), lambda b, h, i: (b * nq + i, gcb + h)),
            pl.BlockSpec((1, LANES), lambda b, h, i: (0, 0)),
        ],
        out_specs=pl.BlockSpec((tq, LANES), lambda b, h, i: (b * nq + i, h)),
        out_shape=jax.ShapeDtypeStruct((t, w), BF16),
        scratch_shapes=[pltpu.VMEM((2 * tq, LANES), BF16), pltpu.VMEM((seq, 2 * LANES), BF16),
                        sbuf, sbuf, sbuf, pbuf, pbuf, pbuf, stat, stat, stat, stat,
                        pltpu.VMEM((2 * tq, 2 * LANES), F32)],
        compiler_params=_params("parallel", "parallel", "arbitrary"),
        name="diff_attention",
    )(lam.reshape(1), qn, kn, vb, proj, subln_g.astype(F32).reshape(1, LANES))


def _proj_res_kernel(a_ref, b_ref, wa_ref, wb_ref, x_ref, o_ref):
    y = (jnp.dot(a_ref[...], wa_ref[...], preferred_element_type=F32)
         + jnp.dot(b_ref[...], wb_ref[...], preferred_element_type=F32))
    o_ref[...] = x_ref[...] + y


def _proj_residual(a, a_cb, b, b_cb, w, x, *, tm, tn):
    m, n = x.shape
    kh = w.shape[0] // 2
    return pl.pallas_call(
        _proj_res_kernel,
        grid=(m // tm, n // tn),
        in_specs=[
            pl.BlockSpec((tm, kh), lambda i, j: (i, a_cb)),
            pl.BlockSpec((tm, kh), lambda i, j: (i, b_cb)),
            pl.BlockSpec((kh, tn), lambda i, j: (0, j)),
            pl.BlockSpec((kh, tn), lambda i, j: (1, j)),
            pl.BlockSpec((tm, tn), lambda i, j: (i, j)),
        ],
        out_specs=pl.BlockSpec((tm, tn), lambda i, j: (i, j)),
        out_shape=jax.ShapeDtypeStruct((m, n), F32),
        compiler_params=_params("parallel", "arbitrary"),
        name="proj_residual",
    )(a, b, w, w, x)


def _split_bf16(x):
    hi = x.astype(BF16)
    return hi, (x - hi.astype(F32)).astype(BF16)


def _s5_kernel(ut_ref, sw_ref, cc_ref, cw_ref, lr_ref, li_ref, y_ref, toep_ref, *, chunks_per_seq):
    p = S5_STATE
    nl, gb, mm, ncol = ut_ref.shape
    width = nl * mm
    reps = ncol // LANES
    lane = lax.broadcasted_iota(jnp.int32, (p, ncol), 1)
    cpos = lane % chunks_per_seq

    def shifted(x, k):
        return jnp.where(cpos >= k, pltpu.roll(x, k, 1), 0.0)

    for g in range(gb):
        u = ut_ref[:, g].reshape(width, ncol)
        sw_hi, sw_lo = _split_bf16(sw_ref[g])
        c_hi, c_lo = _split_bf16(cc_ref[g])
        kcat = (jnp.dot(c_hi, sw_hi, preferred_element_type=F32)
                + jnp.dot(c_hi, sw_lo, preferred_element_type=F32)
                + jnp.dot(c_lo, sw_hi, preferred_element_type=F32))
        z = jnp.concatenate([kcat, jnp.zeros_like(kcat)], axis=1)
        rolled = [z if r == 0 else pltpu.roll(z, 2 * width - r, 1) for r in range(0, LANES, mm)]
        for l in range(nl):
            off = (nl - 1 - l) * mm
            base = off - off % LANES
            strip = rolled[(off % LANES) // mm][:, base:base + width]
            toep_ref[l * mm:(l + 1) * mm, :] = strip.astype(toep_ref.dtype)
        s = jnp.dot(sw_hi, u, preferred_element_type=F32)
        xr, xi = s[:p], s[p:]
        lr = jnp.concatenate([lr_ref[g]] * reps, axis=1)
        li = jnp.concatenate([li_ref[g]] * reps, axis=1)
        k = 1
        while k < chunks_per_seq:
            sr, si = shifted(xr, k), shifted(xi, k)
            xr, xi = xr + (lr * sr - li * si), xi + (lr * si + li * sr)
            lr, li = lr * lr - li * li, 2.0 * (lr * li)
            k *= 2
        h = jnp.concatenate([shifted(xr, 1), shifted(xi, 1)], axis=0).astype(BF16)
        y = (jnp.dot(toep_ref[...], u, preferred_element_type=F32)
             + jnp.dot(cw_ref[g], h, preferred_element_type=F32))
        y_ref[:, g] = y.reshape(nl, mm, ncol)


def _s5_tables(a_re, a_im, log_dt, b_re, b_im, c_re, c_im, chunk):
    g, p, m = b_re.shape
    dt = jnp.exp(log_dt.astype(F32))[:, None]
    a_re, a_im = a_re.astype(F32), a_im.astype(F32)
    mag = jnp.exp(a_re * dt)
    lb_re, lb_im = mag * jnp.cos(a_im * dt), mag * jnp.sin(a_im * dt)
    den = a_re * a_re + a_im * a_im
    nr, ni = lb_re - 1.0, lb_im
    fr = (nr * a_re + ni * a_im) / den
    fi = (ni * a_re - nr * a_im) / den
    bb_re = fr[..., None] * b_re - fi[..., None] * b_im
    bb_im = fr[..., None] * b_im + fi[..., None] * b_re
    d = jnp.arange(chunk + 1, dtype=F32)[None, :, None]
    pmag = jnp.exp((a_re * dt)[:, None, :] * d)
    pang = (a_im * dt)[:, None, :] * d
    pr, pi = pmag * jnp.cos(pang), pmag * jnp.sin(pang)
    c_re, c_im = c_re.astype(F32), c_im.astype(F32)
    rev = chunk - 1 - jnp.arange(chunk)
    prt = jnp.repeat(pr[:, rev].transpose(0, 2, 1), m, axis=2)
    pit = jnp.repeat(pi[:, rev].transpose(0, 2, 1), m, axis=2)
    bbr, bbi = jnp.tile(bb_re, (1, 1, chunk)), jnp.tile(bb_im, (1, 1, chunk))
    sw = jnp.concatenate([prt * bbr - pit * bbi, prt * bbi + pit * bbr], axis=1)
    cc = jnp.concatenate([c_re, -c_im], axis=-1)
    pr1, pi1 = pr[:, 1:, None, :], pi[:, 1:, None, :]
    cr, ci = c_re[:, None], c_im[:, None]
    cw = jnp.concatenate([cr * pr1 - ci * pi1, -(cr * pi1 + ci * pr1)], axis=-1)
    cw = cw.reshape(g, chunk * m, 2 * p)
    lam_r = jnp.broadcast_to(pr[:, chunk, :, None], (g, p, LANES))
    lam_i = jnp.broadcast_to(pi[:, chunk, :, None], (g, p, LANES))
    return sw, cc, cw.astype(BF16), lam_r, lam_i


def _s5(u, tables, *, batch, seq):
    sw, cc, cw, lam_r, lam_i = tables
    t, e = u.shape
    g, m = cc.shape[0], cc.shape[1]
    chunk = sw.shape[2] // m
    ncol = t // chunk
    gb = S5_GROUPS_PER_STEP
    ut = u.reshape(ncol, chunk * e).T.reshape(chunk, g, m, ncol)
    wspec = lambda a: pl.BlockSpec((gb,) + a.shape[1:], lambda i: (i, 0, 0))
    yt = pl.pallas_call(
        functools.partial(_s5_kernel, chunks_per_seq=seq // chunk),
        grid=(g // gb,),
        in_specs=[pl.BlockSpec((chunk, gb, m, ncol), lambda i: (0, i, 0, 0)),
                  wspec(sw), wspec(cc), wspec(cw), wspec(lam_r), wspec(lam_i)],
        out_specs=pl.BlockSpec((chunk, gb, m, ncol), lambda i: (0, i, 0, 0)),
        out_shape=jax.ShapeDtypeStruct((chunk, g, m, ncol), F32),
        scratch_shapes=[pltpu.VMEM((chunk * m, chunk * m), BF16)],
        compiler_params=_params("parallel"),
        name="s5_chunked",
    )(ut, sw, cc, cw, lam_r, lam_i)
    return yt.reshape(chunk * e, ncol).T.reshape(t, e)


def _glu_kernel(ys_ref, u_ref, gate_ref, d_ref, w_ref, b_ref, o_ref):
    y = ys_ref[...] + d_ref[...] * u_ref[...]
    z = 0.5 * y * (1.0 + jnp.tanh(math.sqrt(2.0 / math.pi) * (y + 0.044715 * (y * y * y))))
    t = jnp.dot(z.astype(BF16), w_ref[...], preferred_element_type=F32) + b_ref[...]
    o_ref[...] = (z * jax.nn.sigmoid(t) * _silu(gate_ref[...])).astype(o_ref.dtype)


def _glu(ys, proj, d_skip, w_glu, b_glu, *, tm):
    t, e = ys.shape
    return pl.pallas_call(
        _glu_kernel,
        grid=(t // tm,),
        in_specs=[
            pl.BlockSpec((tm, e), lambda i: (i, 0)),
            pl.BlockSpec((tm, e), lambda i: (i, 0)),
            pl.BlockSpec((tm, e), lambda i: (i, 1)),
            pl.BlockSpec((1, e), lambda i: (0, 0)),
            pl.BlockSpec((e, e), lambda i: (0, 0)),
            pl.BlockSpec((1, e), lambda i: (0, 0)),
        ],
        out_specs=pl.BlockSpec((tm, e), lambda i: (i, 0)),
        out_shape=jax.ShapeDtypeStruct((t, e), BF16),
        compiler_params=_params("parallel"),
        name="glu_gate",
    )(ys, proj, proj, d_skip.astype(F32).reshape(1, e), w_glu, b_glu.astype(F32).reshape(1, e))


def _even_layer(x, layer_idx, norm_g, w_in, conv_w, conv_b, cln_g, cln_b, qn_g, kn_g,
                lam_q1, lam_k1, lam_q2, lam_k2, subln_g, w_out, *, batch, seq):
    conv_ch = conv_w.shape[1]
    proj = _rms_matmul(x, norm_g, w_in.astype(BF16), tm=1024, tn=1024)
    mix_a = _conv_mixer(proj, conv_w.astype(F32), conv_b.astype(F32), cln_g.astype(F32),
                        cln_b.astype(F32), seq=seq, tt=256)
    qn, kn, vb = _qkv_prep(proj, qn_g, kn_g, seq=seq, tt=512, col0=3 * conv_ch)
    lam_init = 0.8 - 0.6 * math.exp(-0.3 * layer_idx)
    lam = (jnp.exp(jnp.sum(lam_q1.astype(F32) * lam_k1.astype(F32)))
           - jnp.exp(jnp.sum(lam_q2.astype(F32) * lam_k2.astype(F32))) + lam_init)
    qkv_w = qn.shape[1]
    mix_b = _diff_attention(lam, qn, kn, vb, proj, subln_g, batch=batch, seq=seq, tq=512,
                            gate_col0=3 * conv_ch + 3 * qkv_w, post_scale=1.0 - lam_init)
    return _proj_residual(mix_a, 0, mix_b, 0, w_out.astype(BF16), x, tm=1024, tn=1024)


def _odd_layer(x, norm_g, w_in, a_re, a_im, log_dt, b_re, b_im, c_re, c_im, d_skip,
               w_glu, b_glu, w_out, *, batch, seq):
    e = w_glu.shape[0]
    proj = _rms_matmul(x, norm_g, w_in.astype(BF16), tm=1024, tn=1024)
    tables = _s5_tables(a_re, a_im, log_dt, b_re, b_im, c_re, c_im, S5_CHUNK)
    ys = _s5(proj[:, :e].astype(BF16), tables, batch=batch, seq=seq)
    out = _glu(ys, proj, d_skip, w_glu.astype(BF16), b_glu, tm=256)
    return _proj_residual(out, 0, out, 1, w_out.astype(BF16), x, tm=1024, tn=1024)


def kernel(x, e_norm_g, e_w_in, e_conv_w, e_conv_b, e_cln_g, e_cln_b, e_qn_g, e_kn_g, e_lam_q1, e_lam_k1, e_lam_q2, e_lam_k2, e_subln_g, e_w_out, o_norm_g, o_w_in, o_A_re, o_A_im, o_log_dt, o_B_re, o_B_im, o_C_re, o_C_im, o_D, o_w_glu, o_b_glu, o_w_out):
    batch, seq, d_model = x.shape
    depth = e_norm_g.shape[0] + o_norm_g.shape[0]
    h = x.reshape(batch * seq, d_model)
    for layer in range(depth):
        j = layer // 2
        if layer % 2 == 0:
            h = _even_layer(h, layer, e_norm_g[j], e_w_in[j], e_conv_w[j], e_conv_b[j], e_cln_g[j],
                            e_cln_b[j], e_qn_g[j], e_kn_g[j], e_lam_q1[j], e_lam_k1[j], e_lam_q2[j],
                            e_lam_k2[j], e_subln_g[j], e_w_out[j], batch=batch, seq=seq)
        else:
            h = _odd_layer(h, o_norm_g[j], o_w_in[j], o_A_re[j], o_A_im[j], o_log_dt[j], o_B_re[j],
                           o_B_im[j], o_C_re[j], o_C_im[j], o_D[j], o_w_glu[j], o_b_glu[j], o_w_out[j],
                           batch=batch, seq=seq)
    return h.reshape(batch, seq, d_model)
```

```python
import functools
import math

import jax
import jax.numpy as jnp
from jax import lax
from jax.experimental import pallas as pl
from jax.experimental.pallas import tpu as pltpu

F32 = jnp.float32
BF16 = jnp.bfloat16

EPS = 1e-6
ROPE_THETA = 10000.0
CONV_W = 31
DA_HEADS = 8
DA_HEAD_DIM = 64
S5_GROUP = 16
S5_STATE = 64

LANES = 128
CONV_HALO = 32
CONV_ROWS = 32
ATT_ROWS = 64
S5_CHUNK = 32
VMEM_LIMIT = 56 * 1024 * 1024


def _params(*sem):
    return pltpu.CompilerParams(dimension_semantics=sem, vmem_limit_bytes=VMEM_LIMIT)


def _silu(x):
    return x * jax.nn.sigmoid(x)


def _rms_matmul_kernel(x_ref, g_ref, w_ref, o_ref, h_ref):
    @pl.when(pl.program_id(1) == 0)
    def _():
        x = x_ref[...]
        ms = jnp.mean(x * x, axis=-1, keepdims=True)
        h_ref[...] = (x * lax.rsqrt(ms + EPS) * g_ref[...]).astype(h_ref.dtype)

    o_ref[...] = jnp.dot(h_ref[...], w_ref[...], preferred_element_type=F32).astype(o_ref.dtype)


def _rms_matmul(x, g, w, *, tm, tn):
    m, d = x.shape
    n = w.shape[1]
    return pl.pallas_call(
        _rms_matmul_kernel,
        grid=(m // tm, n // tn),
        in_specs=[
            pl.BlockSpec((tm, d), lambda i, j: (i, 0)),
            pl.BlockSpec((1, d), lambda i, j: (0, 0)),
            pl.BlockSpec((d, tn), lambda i, j: (0, j)),
        ],
        out_specs=pl.BlockSpec((tm, tn), lambda i, j: (i, j)),
        out_shape=jax.ShapeDtypeStruct((m, n), F32),
        scratch_shapes=[pltpu.VMEM((tm, d), BF16)],
        compiler_params=_params("parallel", "arbitrary"),
        name="rms_matmul",
    )(x, g.reshape(1, d), w)


def _conv_kernel(val_ref, glu_ref, gate_ref, hval_ref, hglu_ref, w_ref, cb_ref, lg_ref, lb_ref,
                 o_ref, uext_ref, *, tt, tiles_per_seq):
    c = o_ref.shape[-1]
    nlt = c // LANES
    first = (pl.program_id(0) % tiles_per_seq) == 0
    halo = jnp.where(first, 0.0, hval_ref[...] * jax.nn.sigmoid(hglu_ref[...]))
    cur = val_ref[...] * jax.nn.sigmoid(glu_ref[...])
    for j in range(nlt):
        sl = slice(j * LANES, (j + 1) * LANES)
        uext_ref[j, pl.ds(0, CONV_HALO, stride=2), :] = halo[:, sl]
        uext_ref[j, pl.ds(2 * CONV_HALO, tt, stride=2), :] = cur[:, sl]
    lg = lg_ref[...]
    lb = lb_ref[...]

    def chunk(r, carry):
        base = pl.multiple_of(r * CONV_ROWS, CONV_ROWS)
        accs = []
        for j in range(nlt):
            sl = slice(j * LANES, (j + 1) * LANES)
            acc = jnp.broadcast_to(cb_ref[:, sl], (CONV_ROWS, LANES))
            for k in range(CONV_W):
                row = base + (CONV_HALO - CONV_W + 1 + k)
                acc = acc + w_ref[k:k + 1, sl] * uext_ref[j, pl.ds(2 * row, CONV_ROWS, stride=2), :]
            accs.append(acc)
        acc = jnp.concatenate(accs, axis=1)
        mean = jnp.mean(acc, axis=-1, keepdims=True)
        xc = acc - mean
        var = jnp.mean(xc * xc, axis=-1, keepdims=True)
        y = _silu(xc * lax.rsqrt(var + EPS) * lg + lb)
        o_ref[pl.ds(base, CONV_ROWS), :] = (y * _silu(gate_ref[pl.ds(base, CONV_ROWS), :])).astype(o_ref.dtype)
        return carry

    lax.fori_loop(0, tt // CONV_ROWS, chunk, 0, unroll=2)


def _conv_mixer(proj, conv_w, conv_b, ln_g, ln_b, *, seq, tt):
    t = proj.shape[0]
    c = conv_w.shape[1]
    hb = tt // CONV_HALO
    row = lambda a: a.reshape(1, c)
    return pl.pallas_call(
        functools.partial(_conv_kernel, tt=tt, tiles_per_seq=seq // tt),
        grid=(t // tt,),
        in_specs=[
            pl.BlockSpec((tt, c), lambda i: (i, 0)),
            pl.BlockSpec((tt, c), lambda i: (i, 1)),
            pl.BlockSpec((tt, c), lambda i: (i, 2)),
            pl.BlockSpec((CONV_HALO, c), lambda i: (jnp.maximum(i * hb - 1, 0), 0)),
            pl.BlockSpec((CONV_HALO, c), lambda i: (jnp.maximum(i * hb - 1, 0), 1)),
            pl.BlockSpec((CONV_W, c), lambda i: (0, 0)),
            pl.BlockSpec((1, c), lambda i: (0, 0)),
            pl.BlockSpec((1, c), lambda i: (0, 0)),
            pl.BlockSpec((1, c), lambda i: (0, 0)),
        ],
        out_specs=pl.BlockSpec((tt, c), lambda i: (i, 0)),
        out_shape=jax.ShapeDtypeStruct((t, c), BF16),
        scratch_shapes=[pltpu.VMEM((c // LANES, 2 * (CONV_HALO + tt), LANES), F32)],
        compiler_params=_params("parallel"),
        name="conv_mixer",
    )(proj, proj, proj, proj, proj, conv_w, row(conv_b), row(ln_g), row(ln_b))


def _qkv_kernel(q_ref, k_ref, v_ref, cos_ref, sin_ref, qg_ref, kg_ref, ones_ref,
                qo_ref, ko_ref, vo_ref, *, q_scale):
    cos = cos_ref[...]
    sin = sin_ref[...]
    ones = ones_ref[...]
    lane = lax.broadcasted_iota(jnp.int32, cos.shape, 1)
    first_half = (lane % DA_HEAD_DIM) < (DA_HEAD_DIM // 2)

    def prep(x, g, scale):
        x2 = x * x
        hi = x2.astype(BF16)
        lo = (x2 - hi.astype(F32)).astype(BF16)
        ss = (jnp.dot(hi, ones, preferred_element_type=F32)
              + jnp.dot(lo, ones, preferred_element_type=F32))
        y = x * lax.rsqrt(ss * (1.0 / DA_HEAD_DIM) + EPS) * g
        partner = jnp.where(first_half, pltpu.roll(y, LANES - DA_HEAD_DIM // 2, 1),
                            pltpu.roll(y, DA_HEAD_DIM // 2, 1))
        return (y * cos + partner * sin) * scale

    for h in range(q_ref.shape[-1] // LANES):
        sl = slice(h * LANES, (h + 1) * LANES)
        qo_ref[:, sl] = prep(q_ref[:, sl], qg_ref[...], q_scale).astype(qo_ref.dtype)
        ko_ref[:, sl] = prep(k_ref[:, sl], kg_ref[...], 1.0).astype(ko_ref.dtype)
    vo_ref[...] = v_ref[...].astype(vo_ref.dtype)


def _qkv_prep(proj, qn_g, kn_g, *, seq, tt, col0):
    t = proj.shape[0]
    w = DA_HEADS * 2 * DA_HEAD_DIM
    half = DA_HEAD_DIM // 2
    freqs = ROPE_THETA ** (-jnp.arange(half, dtype=F32) / half)
    ang = jnp.arange(seq, dtype=F32)[:, None] * freqs[None, :]
    cos = jnp.tile(jnp.cos(ang), (1, LANES // half))
    sin = jnp.tile(jnp.concatenate([-jnp.sin(ang), jnp.sin(ang)], axis=-1), (1, LANES // DA_HEAD_DIM))
    blk = jnp.arange(LANES) // DA_HEAD_DIM
    ones = (blk[:, None] == blk[None, :]).astype(BF16)
    tile_g = lambda g: jnp.tile(g.astype(F32), LANES // DA_HEAD_DIM).reshape(1, LANES)
    cb = col0 // w
    nseq = seq // tt
    out = jax.ShapeDtypeStruct((t, w), BF16)
    return pl.pallas_call(
        functools.partial(_qkv_kernel, q_scale=math.log2(math.e) * DA_HEAD_DIM ** -0.5),
        grid=(t // tt,),
        in_specs=[
            pl.BlockSpec((tt, w), lambda i: (i, cb)),
            pl.BlockSpec((tt, w), lambda i: (i, cb + 1)),
            pl.BlockSpec((tt, w), lambda i: (i, cb + 2)),
            pl.BlockSpec((tt, LANES), lambda i: (i % nseq, 0)),
            pl.BlockSpec((tt, LANES), lambda i: (i % nseq, 0)),
            pl.BlockSpec((1, LANES), lambda i: (0, 0)),
            pl.BlockSpec((1, LANES), lambda i: (0, 0)),
            pl.BlockSpec((LANES, LANES), lambda i: (0, 0)),
        ],
        out_specs=[pl.BlockSpec((tt, w), lambda i: (i, 0))] * 3,
        out_shape=[out, out, out],
        compiler_params=_params("parallel"),
        name="qkv_prep",
    )(proj, proj, proj, cos, sin, tile_g(qn_g), tile_g(kn_g), ones)


def _attn_kernel(lam_ref, q_ref, k_ref, v_ref, bg_ref, sg_ref, o_ref,
                 qs_ref, vaug_ref, s0_ref, s1_ref, s2_ref, p0_ref, p1_ref, p2_ref, m_ref,
                 a0_ref, a1_ref, a2_ref, acc_ref, *, tq, post_scale):
    i = pl.program_id(2)
    s_refs, p_refs, a_refs = (s0_ref, s1_ref, s2_ref), (p0_ref, p1_ref, p2_ref), (a0_ref, a1_ref, a2_ref)

    @pl.when(i == 0)
    def _():
        vaug_ref[:, 0:LANES] = v_ref[...]
        vaug_ref[:, LANES:2 * LANES] = jnp.ones(v_ref.shape, vaug_ref.dtype)

    q = q_ref[...]
    lane = lax.broadcasted_iota(jnp.int32, q.shape, 1)
    zero = jnp.zeros_like(q)
    qs_ref[0:tq, :] = jnp.where(lane < DA_HEAD_DIM, q, zero)
    qs_ref[tq:2 * tq, :] = jnp.where(lane >= DA_HEAD_DIM, q, zero)
    m_ref[...] = jnp.full(m_ref.shape, -jnp.inf, F32)
    acc_ref[...] = jnp.zeros(acc_ref.shape, F32)
    reps = tq // LANES

    def scores(j, slot):
        start = pl.multiple_of(j * tq, tq)
        s_refs[slot][...] = lax.dot_general(qs_ref[...], k_ref[pl.ds(start, tq), :],
                                            (((1,), (1,)), ((), ())), preferred_element_type=F32)

    def softmax(slot, masked):
        s_ref, p_ref, a_ref = s_refs[slot], p_refs[slot], a_refs[slot]
        for r0 in range(0, 2 * tq, ATT_ROWS):
            rs = slice(r0, r0 + ATT_ROWS)
            s = s_ref[rs, :]
            if masked:
                row = (r0 % tq) + lax.broadcasted_iota(jnp.int32, s.shape, 0)
                col = lax.broadcasted_iota(jnp.int32, s.shape, 1)
                s = jnp.where(col <= row, s, -jnp.inf)
            m_prev = m_ref[rs, :]
            m_new = jnp.maximum(m_prev, jnp.max(s, axis=-1, keepdims=True))
            a_ref[rs, :] = jnp.exp2(m_prev - m_new)
            m_ref[rs, :] = m_new
            p_ref[rs, :] = jnp.exp2(s - jnp.concatenate([m_new] * reps, axis=1)).astype(p_ref.dtype)

    def accumulate(j, slot):
        start = pl.multiple_of(j * tq, tq)
        alpha = a_refs[slot][...]
        acc_ref[...] = jnp.concatenate([alpha, alpha], axis=1) * acc_ref[...] + jnp.dot(
            p_refs[slot][...], vaug_ref[pl.ds(start, tq), :], preferred_element_type=F32)

    scores(0, 0)

    @pl.when(i == 0)
    def _():
        softmax(0, True)
        accumulate(0, 0)

    @pl.when(i > 0)
    def _():
        scores(1, 1)
        softmax(0, False)
        npairs = (i - 1) // 2

        def pair(tt, carry):
            t = 2 * tt
            scores(t + 2, 0)
            softmax(1, False)
            accumulate(t, 0)
            scores(t + 3, 1)
            softmax(0, False)
            accumulate(t + 1, 1)
            return carry

        lax.fori_loop(0, npairs, pair, 0)

        @pl.when(i % 2 == 1)
        def _():
            softmax(1, True)
            accumulate(i - 1, 0)
            accumulate(i, 1)

        @pl.when(i % 2 == 0)
        def _():
            scores(i, 2)
            softmax(1, False)
            accumulate(i - 2, 0)
            softmax(2, True)
            accumulate(i - 1, 1)
            accumulate(i, 2)

    o = acc_ref[:, 0:LANES] / acc_ref[:, LANES:2 * LANES]
    d = o[:tq] - lam_ref[0] * o[tq:]
    ms = jnp.mean(d * d, axis=-1, keepdims=True)
    y = d * lax.rsqrt(ms + EPS) * sg_ref[...] * post_scale
    o_ref[...] = (y * _silu(bg_ref[...])).astype(o_ref.dtype)


def _diff_attention(lam, qn, kn, vb, proj, subln_g, *, batch, seq, tq, gate_col0, post_scale):
    t, w = qn.shape
    nq = seq // tq
    gcb = gate_col0 // LANES
    stat = pltpu.VMEM((2 * tq, LANES), F32)
    sbuf = pltpu.VMEM((2 * tq, tq), F32)
    pbuf = pltpu.VMEM((2 * tq, tq), BF16)
    return pl.pallas_call(
        functools.partial(_attn_kernel, tq=tq, post_scale=post_scale),
        grid=(batch, DA_HEADS, nq),
        in_specs=[
            pl.BlockSpec(memory_space=pltpu.SMEM),
            pl.BlockSpec((tq, LANES), lambda b, h, i: (b * nq + i, h)),
            pl.BlockSpec((seq, LANES), lambda b, h, i: (b, h)),
            pl.BlockSpec((seq, LANES), lambda b, h, i: (b, h)),
            pl.BlockSpec((tq, LANES), lambda b, h, i: (b * nq + i, gcb + h)),
            pl.BlockSpec((1, LANES), lambda b, h, i: (0, 0)),
        ],
        out_specs=pl.BlockSpec((tq, LANES), lambda b, h, i: (b * nq + i, h)),
        out_shape=jax.ShapeDtypeStruct((t, w), BF16),
        scratch_shapes=[pltpu.VMEM((2 * tq, LANES), BF16), pltpu.VMEM((seq, 2 * LANES), BF16),
                        sbuf, sbuf, sbuf, pbuf, pbuf, pbuf, stat, stat, stat, stat,
                        pltpu.VMEM((2 * tq, 2 * LANES), F32)],
        compiler_params=_params("parallel", "parallel", "arbitrary"),
        name="diff_attention",
    )(lam.reshape(1), qn, kn, vb, proj, subln_g.astype(F32).reshape(1, LANES))


def _proj_res_kernel(a_ref, b_ref, wa_ref, wb_ref, x_ref, o_ref):
    y = (jnp.dot(a_ref[...], wa_ref[...], preferred_element_type=F32)
         + jnp.dot(b_ref[...], wb_ref[...], preferred_element_type=F32))
    o_ref[...] = x_ref[...] + y


def _proj_residual(a, a_cb, b, b_cb, w, x, *, tm, tn):
    m, n = x.shape
    kh = w.shape[0] // 2
    return pl.pallas_call(
        _proj_res_kernel,
        grid=(m // tm, n // tn),
        in_specs=[
            pl.BlockSpec((tm, kh), lambda i, j: (i, a_cb)),
            pl.BlockSpec((tm, kh), lambda i, j: (i, b_cb)),
            pl.BlockSpec((kh, tn), lambda i, j: (0, j)),
            pl.BlockSpec((kh, tn), lambda i, j: (1, j)),
            pl.BlockSpec((tm, tn), lambda i, j: (i, j)),
        ],
        out_specs=pl.BlockSpec((tm, tn), lambda i, j: (i, j)),
        out_shape=jax.ShapeDtypeStruct((m, n), F32),
        compiler_params=_params("parallel", "arbitrary"),
        name="proj_residual",
    )(a, b, w, w, x)


def _split_bf16(x):
    hi = x.astype(BF16)
    return hi, (x - hi.astype(F32)).astype(BF16)


def _s5_kernel(u_ref, sw_ref, cc_ref, cw_ref, lr_ref, li_ref, y_ref, ut_ref, yt_ref, toep_ref,
               *, chunks_per_seq):
    p = S5_STATE
    gb, width, ncol = ut_ref.shape
    mm = cc_ref.shape[1]
    nl = width // mm
    nc = chunks_per_seq
    reps = ncol // LANES
    lane = lax.broadcasted_iota(jnp.int32, (p, ncol), 1)
    cpos = lane % nc

    def shifted(x, k):
        return jnp.where(cpos >= k, pltpu.roll(x, k, 1), 0.0)

    for b in range(ncol // nc):
        for l in range(nl):
            xt = u_ref[(b * nl + l) * nc:(b * nl + l + 1) * nc, :].T.astype(ut_ref.dtype)
            for g in range(gb):
                ut_ref[g, l * mm:(l + 1) * mm, b * nc:(b + 1) * nc] = xt[g * mm:(g + 1) * mm, :]

    for g in range(gb):
        u = ut_ref[g]
        sw_hi, sw_lo = _split_bf16(sw_ref[g])
        c_hi, c_lo = _split_bf16(cc_ref[g])
        kcat = (jnp.dot(c_hi, sw_hi, preferred_element_type=F32)
                + jnp.dot(c_hi, sw_lo, preferred_element_type=F32)
                + jnp.dot(c_lo, sw_hi, preferred_element_type=F32))
        z = jnp.concatenate([kcat, jnp.zeros_like(kcat)], axis=1)
        rolled = [z if r == 0 else pltpu.roll(z, 2 * width - r, 1) for r in range(0, LANES, mm)]
        for l in range(nl):
            off = (nl - 1 - l) * mm
            base = off - off % LANES
            strip = rolled[(off % LANES) // mm][:, base:base + width]
            toep_ref[l * mm:(l + 1) * mm, :] = strip.astype(toep_ref.dtype)
        s = jnp.dot(sw_hi, u, preferred_element_type=F32)
        xr, xi = s[:p], s[p:]
        lr = jnp.concatenate([lr_ref[g]] * reps, axis=1)
        li = jnp.concatenate([li_ref[g]] * reps, axis=1)
        k = 1
        while k < chunks_per_seq:
            sr, si = shifted(xr, k), shifted(xi, k)
            xr, xi = xr + (lr * sr - li * si), xi + (lr * si + li * sr)
            lr, li = lr * lr - li * li, 2.0 * (lr * li)
            k *= 2
        h = jnp.concatenate([shifted(xr, 1), shifted(xi, 1)], axis=0).astype(BF16)
        yt_ref[g] = (jnp.dot(toep_ref[...], u, preferred_element_type=F32)
                     + jnp.dot(cw_ref[g], h, preferred_element_type=F32))

    for b in range(ncol // nc):
        for l in range(nl):
            zt = jnp.concatenate([yt_ref[g, l * mm:(l + 1) * mm, b * nc:(b + 1) * nc] for g in range(gb)], axis=0)
            y_ref[(b * nl + l) * nc:(b * nl + l + 1) * nc, :] = zt.T


def _s5_tables(a_re, a_im, log_dt, b_re, b_im, c_re, c_im, chunk):
    g, p, m = b_re.shape
    dt = jnp.exp(log_dt.astype(F32))[:, None]
    a_re, a_im = a_re.astype(F32), a_im.astype(F32)
    mag = jnp.exp(a_re * dt)
    lb_re, lb_im = mag * jnp.cos(a_im * dt), mag * jnp.sin(a_im * dt)
    den = a_re * a_re + a_im * a_im
    nr, ni = lb_re - 1.0, lb_im
    fr = (nr * a_re + ni * a_im) / den
    fi = (ni * a_re - nr * a_im) / den
    bb_re = fr[..., None] * b_re - fi[..., None] * b_im
    bb_im = fr[..., None] * b_im + fi[..., None] * b_re
    d = jnp.arange(chunk + 1, dtype=F32)[None, :, None]
    pmag = jnp.exp((a_re * dt)[:, None, :] * d)
    pang = (a_im * dt)[:, None, :] * d
    pr, pi = pmag * jnp.cos(pang), pmag * jnp.sin(pang)
    c_re, c_im = c_re.astype(F32), c_im.astype(F32)
    rev = chunk - 1 - jnp.arange(chunk)
    prt = jnp.repeat(pr[:, rev].transpose(0, 2, 1), m, axis=2)
    pit = jnp.repeat(pi[:, rev].transpose(0, 2, 1), m, axis=2)
    bbr, bbi = jnp.tile(bb_re, (1, 1, chunk)), jnp.tile(bb_im, (1, 1, chunk))
    sw = jnp.concatenate([prt * bbr - pit * bbi, prt * bbi + pit * bbr], axis=1)
    cc = jnp.concatenate([c_re, -c_im], axis=-1)
    pr1, pi1 = pr[:, 1:, None, :], pi[:, 1:, None, :]
    cr, ci = c_re[:, None], c_im[:, None]
    cw = jnp.concatenate([cr * pr1 - ci * pi1, -(cr * pi1 + ci * pr1)], axis=-1)
    cw = cw.reshape(g, chunk * m, 2 * p)
    lam_r = jnp.broadcast_to(pr[:, chunk, :, None], (g, p, LANES))
    lam_i = jnp.broadcast_to(pi[:, chunk, :, None], (g, p, LANES))
    return sw, cc, cw.astype(BF16), lam_r, lam_i


def _s5(proj, tables, *, batch, seq):
    sw, cc, cw, lam_r, lam_i = tables
    t = proj.shape[0]
    g, m = cc.shape[0], cc.shape[1]
    e = g * m
    chunk = sw.shape[2] // m
    nc = seq // chunk
    ncol = t // chunk
    gb = LANES // m
    up = proj.reshape(batch, nc, chunk, proj.shape[1])[..., :e].transpose(0, 2, 1, 3).reshape(t, e)
    wspec = lambda a: pl.BlockSpec((gb,) + a.shape[1:], lambda i: (i, 0, 0))
    yp = pl.pallas_call(
        functools.partial(_s5_kernel, chunks_per_seq=nc),
        grid=(g // gb,),
        in_specs=[pl.BlockSpec((t, LANES), lambda i: (0, i)),
                  wspec(sw), wspec(cc), wspec(cw), wspec(lam_r), wspec(lam_i)],
        out_specs=pl.BlockSpec((t, LANES), lambda i: (0, i)),
        out_shape=jax.ShapeDtypeStruct((t, e), F32),
        scratch_shapes=[pltpu.VMEM((gb, chunk * m, ncol), BF16), pltpu.VMEM((gb, chunk * m, ncol), F32),
                        pltpu.VMEM((chunk * m, chunk * m), BF16)],
        compiler_params=_params("parallel"),
        name="s5_chunked",
    )(up, sw, cc, cw, lam_r, lam_i)
    return yp.reshape(batch, chunk, nc, e).transpose(0, 2, 1, 3).reshape(t, e)


def _glu_kernel(ys_ref, u_ref, gate_ref, d_ref, w_ref, b_ref, o_ref):
    y = ys_ref[...] + d_ref[...] * u_ref[...]
    z = 0.5 * y * (1.0 + jnp.tanh(math.sqrt(2.0 / math.pi) * (y + 0.044715 * (y * y * y))))
    t = jnp.dot(z.astype(BF16), w_ref[...], preferred_element_type=F32) + b_ref[...]
    o_ref[...] = (z * jax.nn.sigmoid(t) * _silu(gate_ref[...])).astype(o_ref.dtype)


def _glu(ys, proj, d_skip, w_glu, b_glu, *, tm):
    t, e = ys.shape
    return pl.pallas_call(
        _glu_kernel,
        grid=(t // tm,),
        in_specs=[
            pl.BlockSpec((tm, e), lambda i: (i, 0)),
            pl.BlockSpec((tm, e), lambda i: (i, 0)),
            pl.BlockSpec((tm, e), lambda i: (i, 1)),
            pl.BlockSpec((1, e), lambda i: (0, 0)),
            pl.BlockSpec((e, e), lambda i: (0, 0)),
            pl.BlockSpec((1, e), lambda i: (0, 0)),
        ],
        out_specs=pl.BlockSpec((tm, e), lambda i: (i, 0)),
        out_shape=jax.ShapeDtypeStruct((t, e), BF16),
        compiler_params=_params("parallel"),
        name="glu_gate",
    )(ys, proj, proj, d_skip.astype(F32).reshape(1, e), w_glu, b_glu.astype(F32).reshape(1, e))


def _even_layer(x, layer_idx, norm_g, w_in, conv_w, conv_b, cln_g, cln_b, qn_g, kn_g,
                lam_q1, lam_k1, lam_q2, lam_k2, subln_g, w_out, *, batch, seq):
    conv_ch = conv_w.shape[1]
    proj = _rms_matmul(x, norm_g, w_in.astype(BF16), tm=1024, tn=1024)
    mix_a = _conv_mixer(proj, conv_w.astype(F32), conv_b.astype(F32), cln_g.astype(F32),
                        cln_b.astype(F32), seq=seq, tt=256)
    qn, kn, vb = _qkv_prep(proj, qn_g, kn_g, seq=seq, tt=512, col0=3 * conv_ch)
    lam_init = 0.8 - 0.6 * math.exp(-0.3 * layer_idx)
    lam = (jnp.exp(jnp.sum(lam_q1.astype(F32) * lam_k1.astype(F32)))
           - jnp.exp(jnp.sum(lam_q2.astype(F32) * lam_k2.astype(F32))) + lam_init)
    qkv_w = qn.shape[1]
    mix_b = _diff_attention(lam, qn, kn, vb, proj, subln_g, batch=batch, seq=seq, tq=512,
                            gate_col0=3 * conv_ch + 3 * qkv_w, post_scale=1.0 - lam_init)
    return _proj_residual(mix_a, 0, mix_b, 0, w_out.astype(BF16), x, tm=1024, tn=1024)


def _odd_layer(x, norm_g, w_in, a_re, a_im, log_dt, b_re, b_im, c_re, c_im, d_skip,
               w_glu, b_glu, w_out, *, batch, seq):
    e = w_glu.shape[0]
    proj = _rms_matmul(x, norm_g, w_in.astype(BF16), tm=1024, tn=1024)
    tables = _s5_tables(a_re, a_im, log_dt, b_re, b_im, c_re, c_im, S5_CHUNK)
    ys = _s5(proj, tables, batch=batch, seq=seq)
    out = _glu(ys, proj, d_skip, w_glu.astype(BF16), b_glu, tm=256)
    return _proj_residual(out, 0, out, 1, w_out.astype(BF16), x, tm=1024, tn=1024)


def kernel(x, e_norm_g, e_w_in, e_conv_w, e_conv_b, e_cln_g, e_cln_b, e_qn_g, e_kn_g, e_lam_q1, e_lam_k1, e_lam_q2, e_lam_k2, e_subln_g, e_w_out, o_norm_g, o_w_in, o_A_re, o_A_im, o_log_dt, o_B_re, o_B_im, o_C_re, o_C_im, o_D, o_w_glu, o_b_glu, o_w_out):
    batch, seq, d_model = x.shape
    depth = e_norm_g.shape[0] + o_norm_g.shape[0]
    h = x.reshape(batch * seq, d_model)
    for layer in range(depth):
        j = layer // 2
        if layer % 2 == 0:
            h = _even_layer(h, layer, e_norm_g[j], e_w_in[j], e_conv_w[j], e_conv_b[j], e_cln_g[j],
                            e_cln_b[j], e_qn_g[j], e_kn_g[j], e_lam_q1[j], e_lam_k1[j], e_lam_q2[j],
                            e_lam_k2[j], e_subln_g[j], e_w_out[j], batch=batch, seq=seq)
        else:
            h = _odd_layer(h, o_norm_g[j], o_w_in[j], o_A_re[j], o_A_im[j], o_log_dt[j], o_B_re[j],
                           o_B_im[j], o_C_re[j], o_C_im[j], o_D[j], o_w_glu[j], o_b_glu[j], o_w_out[j],
                           batch=batch, seq=seq)
    return h.reshape(batch, seq, d_model)
```

```python
import functools
import math

import jax
import jax.numpy as jnp
from jax import lax
from jax.experimental import pallas as pl
from jax.experimental.pallas import tpu as pltpu

F32 = jnp.float32
BF16 = jnp.bfloat16

EPS = 1e-6
ROPE_THETA = 10000.0
CONV_W = 31
DA_HEADS = 8
DA_HEAD_DIM = 64
S5_GROUP = 16
S5_STATE = 64

LANES = 128
CONV_HALO = 32
CONV_ROWS = 32
ATT_ROWS = 64
S5_CHUNK = 32
VMEM_LIMIT = 56 * 1024 * 1024


def _params(*sem):
    return pltpu.CompilerParams(dimension_semantics=sem, vmem_limit_bytes=VMEM_LIMIT)


def _silu(x):
    return x * jax.nn.sigmoid(x)


def _rms_matmul_kernel(x_ref, g_ref, w_ref, o_ref, h_ref):
    @pl.when(pl.program_id(1) == 0)
    def _():
        x = x_ref[...]
        ms = jnp.mean(x * x, axis=-1, keepdims=True)
        h_ref[...] = (x * lax.rsqrt(ms + EPS) * g_ref[...]).astype(h_ref.dtype)

    o_ref[...] = jnp.dot(h_ref[...], w_ref[...], preferred_element_type=F32).astype(o_ref.dtype)


def _rms_matmul(x, g, w, *, tm, tn):
    m, d = x.shape
    n = w.shape[1]
    return pl.pallas_call(
        _rms_matmul_kernel,
        grid=(m // tm, n // tn),
        in_specs=[
            pl.BlockSpec((tm, d), lambda i, j: (i, 0)),
            pl.BlockSpec((1, d), lambda i, j: (0, 0)),
            pl.BlockSpec((d, tn), lambda i, j: (0, j)),
        ],
        out_specs=pl.BlockSpec((tm, tn), lambda i, j: (i, j)),
        out_shape=jax.ShapeDtypeStruct((m, n), BF16),
        scratch_shapes=[pltpu.VMEM((tm, d), BF16)],
        compiler_params=_params("parallel", "arbitrary"),
        name="rms_matmul",
    )(x, g.reshape(1, d), w)


def _conv_kernel(val_ref, glu_ref, gate_ref, hval_ref, hglu_ref, w_ref, cb_ref, lg_ref, lb_ref,
                 o_ref, uext_ref, *, tt, tiles_per_seq):
    c = o_ref.shape[-1]
    nlt = c // LANES
    first = (pl.program_id(0) % tiles_per_seq) == 0
    f32 = lambda ref, idx=Ellipsis: ref[idx].astype(F32)
    halo = jnp.where(first, 0.0, f32(hval_ref) * jax.nn.sigmoid(f32(hglu_ref)))
    cur = f32(val_ref) * jax.nn.sigmoid(f32(glu_ref))
    for j in range(nlt):
        sl = slice(j * LANES, (j + 1) * LANES)
        uext_ref[j, pl.ds(0, CONV_HALO, stride=2), :] = halo[:, sl]
        uext_ref[j, pl.ds(2 * CONV_HALO, tt, stride=2), :] = cur[:, sl]
    lg = lg_ref[...]
    lb = lb_ref[...]

    def chunk(r, carry):
        base = pl.multiple_of(r * CONV_ROWS, CONV_ROWS)
        accs = []
        for j in range(nlt):
            sl = slice(j * LANES, (j + 1) * LANES)
            acc = jnp.broadcast_to(cb_ref[:, sl], (CONV_ROWS, LANES))
            for k in range(CONV_W):
                row = base + (CONV_HALO - CONV_W + 1 + k)
                acc = acc + w_ref[k:k + 1, sl] * uext_ref[j, pl.ds(2 * row, CONV_ROWS, stride=2), :]
            accs.append(acc)
        acc = jnp.concatenate(accs, axis=1)
        mean = jnp.mean(acc, axis=-1, keepdims=True)
        xc = acc - mean
        var = jnp.mean(xc * xc, axis=-1, keepdims=True)
        y = _silu(xc * lax.rsqrt(var + EPS) * lg + lb)
        gate = gate_ref[pl.ds(base, CONV_ROWS), :].astype(F32)
        o_ref[pl.ds(base, CONV_ROWS), :] = (y * _silu(gate)).astype(o_ref.dtype)
        return carry

    lax.fori_loop(0, tt // CONV_ROWS, chunk, 0, unroll=2)


def _conv_mixer(proj, conv_w, conv_b, ln_g, ln_b, *, seq, tt):
    t = proj.shape[0]
    c = conv_w.shape[1]
    hb = tt // CONV_HALO
    row = lambda a: a.reshape(1, c)
    return pl.pallas_call(
        functools.partial(_conv_kernel, tt=tt, tiles_per_seq=seq // tt),
        grid=(t // tt,),
        in_specs=[
            pl.BlockSpec((tt, c), lambda i: (i, 0)),
            pl.BlockSpec((tt, c), lambda i: (i, 1)),
            pl.BlockSpec((tt, c), lambda i: (i, 2)),
            pl.BlockSpec((CONV_HALO, c), lambda i: (jnp.maximum(i * hb - 1, 0), 0)),
            pl.BlockSpec((CONV_HALO, c), lambda i: (jnp.maximum(i * hb - 1, 0), 1)),
            pl.BlockSpec((CONV_W, c), lambda i: (0, 0)),
            pl.BlockSpec((1, c), lambda i: (0, 0)),
            pl.BlockSpec((1, c), lambda i: (0, 0)),
            pl.BlockSpec((1, c), lambda i: (0, 0)),
        ],
        out_specs=pl.BlockSpec((tt, c), lambda i: (i, 0)),
        out_shape=jax.ShapeDtypeStruct((t, c), BF16),
        scratch_shapes=[pltpu.VMEM((c // LANES, 2 * (CONV_HALO + tt), LANES), F32)],
        compiler_params=_params("parallel"),
        name="conv_mixer",
    )(proj, proj, proj, proj, proj, conv_w, row(conv_b), row(ln_g), row(ln_b))


def _qkv_kernel(q_ref, k_ref, cos_ref, sin_ref, qg_ref, kg_ref, ones_ref, qo_ref, ko_ref, *, q_scale):
    cos = cos_ref[...]
    sin = sin_ref[...]
    ones = ones_ref[...]
    lane = lax.broadcasted_iota(jnp.int32, cos.shape, 1)
    first_half = (lane % DA_HEAD_DIM) < (DA_HEAD_DIM // 2)

    def prep(x, g, scale):
        x2 = x * x
        hi = x2.astype(BF16)
        lo = (x2 - hi.astype(F32)).astype(BF16)
        ss = (jnp.dot(hi, ones, preferred_element_type=F32)
              + jnp.dot(lo, ones, preferred_element_type=F32))
        y = x * lax.rsqrt(ss * (1.0 / DA_HEAD_DIM) + EPS) * g
        partner = jnp.where(first_half, pltpu.roll(y, LANES - DA_HEAD_DIM // 2, 1),
                            pltpu.roll(y, DA_HEAD_DIM // 2, 1))
        return (y * cos + partner * sin) * scale

    for h in range(q_ref.shape[-1] // LANES):
        sl = slice(h * LANES, (h + 1) * LANES)
        qo_ref[:, sl] = prep(q_ref[:, sl].astype(F32), qg_ref[...], q_scale).astype(qo_ref.dtype)
        ko_ref[:, sl] = prep(k_ref[:, sl].astype(F32), kg_ref[...], 1.0).astype(ko_ref.dtype)


def _qkv_prep(proj, qn_g, kn_g, *, seq, tt, col0):
    t = proj.shape[0]
    w = DA_HEADS * 2 * DA_HEAD_DIM
    half = DA_HEAD_DIM // 2
    freqs = ROPE_THETA ** (-jnp.arange(half, dtype=F32) / half)
    ang = jnp.arange(seq, dtype=F32)[:, None] * freqs[None, :]
    cos = jnp.tile(jnp.cos(ang), (1, LANES // half))
    sin = jnp.tile(jnp.concatenate([-jnp.sin(ang), jnp.sin(ang)], axis=-1), (1, LANES // DA_HEAD_DIM))
    blk = jnp.arange(LANES) // DA_HEAD_DIM
    ones = (blk[:, None] == blk[None, :]).astype(BF16)
    tile_g = lambda g: jnp.tile(g.astype(F32), LANES // DA_HEAD_DIM).reshape(1, LANES)
    cb = col0 // w
    nseq = seq // tt
    out = jax.ShapeDtypeStruct((t, w), BF16)
    return pl.pallas_call(
        functools.partial(_qkv_kernel, q_scale=math.log2(math.e) * DA_HEAD_DIM ** -0.5),
        grid=(t // tt,),
        in_specs=[
            pl.BlockSpec((tt, w), lambda i: (i, cb)),
            pl.BlockSpec((tt, w), lambda i: (i, cb + 1)),
            pl.BlockSpec((tt, LANES), lambda i: (i % nseq, 0)),
            pl.BlockSpec((tt, LANES), lambda i: (i % nseq, 0)),
            pl.BlockSpec((1, LANES), lambda i: (0, 0)),
            pl.BlockSpec((1, LANES), lambda i: (0, 0)),
            pl.BlockSpec((LANES, LANES), lambda i: (0, 0)),
        ],
        out_specs=[pl.BlockSpec((tt, w), lambda i: (i, 0))] * 2,
        out_shape=[out, out],
        compiler_params=_params("parallel"),
        name="qkv_prep",
    )(proj, proj, cos, sin, tile_g(qn_g), tile_g(kn_g), ones)


def _attn_kernel(lam_ref, q_ref, k_ref, v_ref, bg_ref, sg_ref, o_ref,
                 qs_ref, vaug_ref, s0_ref, s1_ref, s2_ref, p0_ref, p1_ref, p2_ref, m_ref,
                 a0_ref, a1_ref, a2_ref, acc_ref, *, tq, post_scale):
    i = pl.program_id(2)
    s_refs, p_refs, a_refs = (s0_ref, s1_ref, s2_ref), (p0_ref, p1_ref, p2_ref), (a0_ref, a1_ref, a2_ref)

    @pl.when(i == 0)
    def _():
        vaug_ref[:, 0:LANES] = v_ref[...]
        vaug_ref[:, LANES:2 * LANES] = jnp.ones(v_ref.shape, vaug_ref.dtype)

    q = q_ref[...]
    lane = lax.broadcasted_iota(jnp.int32, q.shape, 1)
    zero = jnp.zeros_like(q)
    qs_ref[0:tq, :] = jnp.where(lane < DA_HEAD_DIM, q, zero)
    qs_ref[tq:2 * tq, :] = jnp.where(lane >= DA_HEAD_DIM, q, zero)
    m_ref[...] = jnp.full(m_ref.shape, -jnp.inf, F32)
    acc_ref[...] = jnp.zeros(acc_ref.shape, F32)
    reps = tq // LANES

    def scores(j, slot):
        start = pl.multiple_of(j * tq, tq)
        s_refs[slot][...] = lax.dot_general(qs_ref[...], k_ref[pl.ds(start, tq), :],
                                            (((1,), (1,)), ((), ())), preferred_element_type=F32)

    def softmax(slot, masked):
        s_ref, p_ref, a_ref = s_refs[slot], p_refs[slot], a_refs[slot]
        for r0 in range(0, 2 * tq, ATT_ROWS):
            rs = slice(r0, r0 + ATT_ROWS)
            s = s_ref[rs, :]
            if masked:
                row = (r0 % tq) + lax.broadcasted_iota(jnp.int32, s.shape, 0)
                col = lax.broadcasted_iota(jnp.int32, s.shape, 1)
                s = jnp.where(col <= row, s, -jnp.inf)
            m_prev = m_ref[rs, :]
            m_new = jnp.maximum(m_prev, jnp.max(s, axis=-1, keepdims=True))
            a_ref[rs, :] = jnp.exp2(m_prev - m_new)
            m_ref[rs, :] = m_new
            p_ref[rs, :] = jnp.exp2(s - jnp.concatenate([m_new] * reps, axis=1)).astype(p_ref.dtype)

    def accumulate(j, slot):
        start = pl.multiple_of(j * tq, tq)
        alpha = a_refs[slot][...]
        acc_ref[...] = jnp.concatenate([alpha, alpha], axis=1) * acc_ref[...] + jnp.dot(
            p_refs[slot][...], vaug_ref[pl.ds(start, tq), :], preferred_element_type=F32)

    scores(0, 0)

    @pl.when(i == 0)
    def _():
        softmax(0, True)
        accumulate(0, 0)

    @pl.when(i > 0)
    def _():
        scores(1, 1)
        softmax(0, False)
        npairs = (i - 1) // 2

        def pair(tt, carry):
            t = 2 * tt
            scores(t + 2, 0)
            softmax(1, False)
            accumulate(t, 0)
            scores(t + 3, 1)
            softmax(0, False)
            accumulate(t + 1, 1)
            return carry

        lax.fori_loop(0, npairs, pair, 0)

        @pl.when(i % 2 == 1)
        def _():
            softmax(1, True)
            accumulate(i - 1, 0)
            accumulate(i, 1)

        @pl.when(i % 2 == 0)
        def _():
            scores(i, 2)
            softmax(1, False)
            accumulate(i - 2, 0)
            softmax(2, True)
            accumulate(i - 1, 1)
            accumulate(i, 2)

    o = acc_ref[:, 0:LANES] / acc_ref[:, LANES:2 * LANES]
    d = o[:tq] - lam_ref[0] * o[tq:]
    ms = jnp.mean(d * d, axis=-1, keepdims=True)
    y = d * lax.rsqrt(ms + EPS) * sg_ref[...] * post_scale
    o_ref[...] = (y * _silu(bg_ref[...].astype(F32))).astype(o_ref.dtype)


def _diff_attention(lam, qn, kn, proj, subln_g, *, batch, seq, tq, v_col0, gate_col0, post_scale):
    t, w = qn.shape
    nq = seq // tq
    vcb = v_col0 // LANES
    gcb = gate_col0 // LANES
    stat = pltpu.VMEM((2 * tq, LANES), F32)
    sbuf = pltpu.VMEM((2 * tq, tq), F32)
    pbuf = pltpu.VMEM((2 * tq, tq), BF16)
    return pl.pallas_call(
        functools.partial(_attn_kernel, tq=tq, post_scale=post_scale),
        grid=(batch, DA_HEADS, nq),
        in_specs=[
            pl.BlockSpec(memory_space=pltpu.SMEM),
            pl.BlockSpec((tq, LANES), lambda b, h, i: (b * nq + i, h)),
            pl.BlockSpec((seq, LANES), lambda b, h, i: (b, h)),
            pl.BlockSpec((seq, LANES), lambda b, h, i: (b, vcb + h)),
            pl.BlockSpec((tq, LANES), lambda b, h, i: (b * nq + i, gcb + h)),
            pl.BlockSpec((1, LANES), lambda b, h, i: (0, 0)),
        ],
        out_specs=pl.BlockSpec((tq, LANES), lambda b, h, i: (b * nq + i, h)),
        out_shape=jax.ShapeDtypeStruct((t, w), BF16),
        scratch_shapes=[pltpu.VMEM((2 * tq, LANES), BF16), pltpu.VMEM((seq, 2 * LANES), BF16),
                        sbuf, sbuf, sbuf, pbuf, pbuf, pbuf, stat, stat, stat, stat,
                        pltpu.VMEM((2 * tq, 2 * LANES), F32)],
        compiler_params=_params("parallel", "parallel", "arbitrary"),
        name="diff_attention",
    )(lam.reshape(1), qn, kn, proj, proj, subln_g.astype(F32).reshape(1, LANES))


def _proj_res_kernel(a_ref, b_ref, wa_ref, wb_ref, x_ref, o_ref):
    y = (jnp.dot(a_ref[...], wa_ref[...], preferred_element_type=F32)
         + jnp.dot(b_ref[...], wb_ref[...], preferred_element_type=F32))
    o_ref[...] = x_ref[...] + y


def _proj_residual(a, a_cb, b, b_cb, w, x, *, tm, tn):
    m, n = x.shape
    kh = w.shape[0] // 2
    return pl.pallas_call(
        _proj_res_kernel,
        grid=(m // tm, n // tn),
        in_specs=[
            pl.BlockSpec((tm, kh), lambda i, j: (i, a_cb)),
            pl.BlockSpec((tm, kh), lambda i, j: (i, b_cb)),
            pl.BlockSpec((kh, tn), lambda i, j: (0, j)),
            pl.BlockSpec((kh, tn), lambda i, j: (1, j)),
            pl.BlockSpec((tm, tn), lambda i, j: (i, j)),
        ],
        out_specs=pl.BlockSpec((tm, tn), lambda i, j: (i, j)),
        out_shape=jax.ShapeDtypeStruct((m, n), F32),
        compiler_params=_params("parallel", "arbitrary"),
        name="proj_residual",
    )(a, b, w, w, x)


def _split_bf16(x):
    hi = x.astype(BF16)
    return hi, (x - hi.astype(F32)).astype(BF16)


def _s5_kernel(u_ref, sw_ref, cc_ref, cw_ref, lr_ref, li_ref, y_ref, ut_ref, yt_ref, toep_ref,
               *, chunks_per_seq):
    p = S5_STATE
    gb, width, ncol = ut_ref.shape
    mm = cc_ref.shape[1]
    nl = width // mm
    nc = chunks_per_seq
    reps = ncol // LANES
    lane = lax.broadcasted_iota(jnp.int32, (p, ncol), 1)
    cpos = lane % nc

    def shifted(x, k):
        return jnp.where(cpos >= k, pltpu.roll(x, k, 1), 0.0)

    for b in range(ncol // nc):
        for l in range(nl):
            xt = u_ref[(b * nl + l) * nc:(b * nl + l + 1) * nc, :].astype(F32).T.astype(ut_ref.dtype)
            for g in range(gb):
                ut_ref[g, l * mm:(l + 1) * mm, b * nc:(b + 1) * nc] = xt[g * mm:(g + 1) * mm, :]

    for g in range(gb):
        u = ut_ref[g]
        sw_hi, sw_lo = _split_bf16(sw_ref[g])
        c_hi, c_lo = _split_bf16(cc_ref[g])
        kcat = (jnp.dot(c_hi, sw_hi, preferred_element_type=F32)
                + jnp.dot(c_hi, sw_lo, preferred_element_type=F32)
                + jnp.dot(c_lo, sw_hi, preferred_element_type=F32))
        z = jnp.concatenate([kcat, jnp.zeros_like(kcat)], axis=1)
        rolled = [z if r == 0 else pltpu.roll(z, 2 * width - r, 1) for r in range(0, LANES, mm)]
        for l in range(nl):
            off = (nl - 1 - l) * mm
            base = off - off % LANES
            strip = rolled[(off % LANES) // mm][:, base:base + width]
            toep_ref[l * mm:(l + 1) * mm, :] = strip.astype(toep_ref.dtype)
        s = jnp.dot(sw_hi, u, preferred_element_type=F32)
        xr, xi = s[:p], s[p:]
        lr = jnp.concatenate([lr_ref[g]] * reps, axis=1)
        li = jnp.concatenate([li_ref[g]] * reps, axis=1)
        k = 1
        while k < chunks_per_seq:
            sr, si = shifted(xr, k), shifted(xi, k)
            xr, xi = xr + (lr * sr - li * si), xi + (lr * si + li * sr)
            lr, li = lr * lr - li * li, 2.0 * (lr * li)
            k *= 2
        h = jnp.concatenate([shifted(xr, 1), shifted(xi, 1)], axis=0).astype(BF16)
        yt_ref[g] = (jnp.dot(toep_ref[...], u, preferred_element_type=F32)
                     + jnp.dot(cw_ref[g], h, preferred_element_type=F32))

    for b in range(ncol // nc):
        for l in range(nl):
            zt = jnp.concatenate([yt_ref[g, l * mm:(l + 1) * mm, b * nc:(b + 1) * nc] for g in range(gb)], axis=0)
            y_ref[(b * nl + l) * nc:(b * nl + l + 1) * nc, :] = zt.T


def _s5_tables(a_re, a_im, log_dt, b_re, b_im, c_re, c_im, chunk):
    g, p, m = b_re.shape
    dt = jnp.exp(log_dt.astype(F32))[:, None]
    a_re, a_im = a_re.astype(F32), a_im.astype(F32)
    mag = jnp.exp(a_re * dt)
    lb_re, lb_im = mag * jnp.cos(a_im * dt), mag * jnp.sin(a_im * dt)
    den = a_re * a_re + a_im * a_im
    nr, ni = lb_re - 1.0, lb_im
    fr = (nr * a_re + ni * a_im) / den
    fi = (ni * a_re - nr * a_im) / den
    bb_re = fr[..., None] * b_re - fi[..., None] * b_im
    bb_im = fr[..., None] * b_im + fi[..., None] * b_re
    d = jnp.arange(chunk + 1, dtype=F32)[None, :, None]
    pmag = jnp.exp((a_re * dt)[:, None, :] * d)
    pang = (a_im * dt)[:, None, :] * d
    pr, pi = pmag * jnp.cos(pang), pmag * jnp.sin(pang)
    c_re, c_im = c_re.astype(F32), c_im.astype(F32)
    hp = lax.Precision.HIGHEST
    rep_l = jnp.repeat(jnp.eye(chunk, dtype=F32)[::-1], m, axis=1)
    til_m = jnp.tile(jnp.eye(m, dtype=F32), (1, chunk))
    prt = jnp.einsum('gdp,dj->gpj', pr[:, :chunk], rep_l, precision=hp)
    pit = jnp.einsum('gdp,dj->gpj', pi[:, :chunk], rep_l, precision=hp)
    bbr = jnp.einsum('gpm,mj->gpj', bb_re, til_m, precision=hp)
    bbi = jnp.einsum('gpm,mj->gpj', bb_im, til_m, precision=hp)
    sw = jnp.concatenate([prt * bbr - pit * bbi, prt * bbi + pit * bbr], axis=1)
    cc = jnp.concatenate([c_re, -c_im], axis=-1)
    pr1, pi1 = pr[:, 1:, None, :], pi[:, 1:, None, :]
    cr, ci = c_re[:, None], c_im[:, None]
    cw = jnp.concatenate([cr * pr1 - ci * pi1, -(cr * pi1 + ci * pr1)], axis=-1)
    cw = cw.reshape(g, chunk * m, 2 * p)
    lam_r = jnp.broadcast_to(pr[:, chunk, :, None], (g, p, LANES))
    lam_i = jnp.broadcast_to(pi[:, chunk, :, None], (g, p, LANES))
    return sw, cc, cw.astype(BF16), lam_r, lam_i


def _s5(proj, tables, *, batch, seq):
    sw, cc, cw, lam_r, lam_i = tables
    t = proj.shape[0]
    g, m = cc.shape[0], cc.shape[1]
    e = g * m
    chunk = sw.shape[2] // m
    nc = seq // chunk
    ncol = t // chunk
    gb = LANES // m
    up = proj.reshape(batch, nc, chunk, proj.shape[1])[..., :e].transpose(0, 2, 1, 3).reshape(t, e)
    wspec = lambda a: pl.BlockSpec((gb,) + a.shape[1:], lambda i: (i, 0, 0))
    yp = pl.pallas_call(
        functools.partial(_s5_kernel, chunks_per_seq=nc),
        grid=(g // gb,),
        in_specs=[pl.BlockSpec((t, LANES), lambda i: (0, i)),
                  wspec(sw), wspec(cc), wspec(cw), wspec(lam_r), wspec(lam_i)],
        out_specs=pl.BlockSpec((t, LANES), lambda i: (0, i)),
        out_shape=jax.ShapeDtypeStruct((t, e), F32),
        scratch_shapes=[pltpu.VMEM((gb, chunk * m, ncol), BF16), pltpu.VMEM((gb, chunk * m, ncol), F32),
                        pltpu.VMEM((chunk * m, chunk * m), BF16)],
        compiler_params=_params("parallel"),
        name="s5_chunked",
    )(up, sw, cc, cw, lam_r, lam_i)
    return yp.reshape(batch, chunk, nc, e).transpose(0, 2, 1, 3).reshape(t, e)


def _glu_kernel(ys_ref, u_ref, gate_ref, d_ref, w_ref, b_ref, o_ref):
    half = o_ref.shape[0] // 2
    for r0 in (0, half):
        rs = slice(r0, r0 + half)
        y = ys_ref[rs, :] + d_ref[...] * u_ref[rs, :].astype(F32)
        z = 0.5 * y * (1.0 + jnp.tanh(math.sqrt(2.0 / math.pi) * (y + 0.044715 * (y * y * y))))
        t = jnp.dot(z.astype(BF16), w_ref[...], preferred_element_type=F32) + b_ref[...]
        o_ref[rs, :] = (z * jax.nn.sigmoid(t) * _silu(gate_ref[rs, :].astype(F32))).astype(o_ref.dtype)


def _glu(ys, proj, d_skip, w_glu, b_glu, *, tm):
    t, e = ys.shape
    return pl.pallas_call(
        _glu_kernel,
        grid=(t // tm,),
        in_specs=[
            pl.BlockSpec((tm, e), lambda i: (i, 0)),
            pl.BlockSpec((tm, e), lambda i: (i, 0)),
            pl.BlockSpec((tm, e), lambda i: (i, 1)),
            pl.BlockSpec((1, e), lambda i: (0, 0)),
            pl.BlockSpec((e, e), lambda i: (0, 0)),
            pl.BlockSpec((1, e), lambda i: (0, 0)),
        ],
        out_specs=pl.BlockSpec((tm, e), lambda i: (i, 0)),
        out_shape=jax.ShapeDtypeStruct((t, e), BF16),
        compiler_params=_params("parallel"),
        name="glu_gate",
    )(ys, proj, proj, d_skip.astype(F32).reshape(1, e), w_glu, b_glu.astype(F32).reshape(1, e))


def _even_layer(x, layer_idx, norm_g, w_in, conv_w, conv_b, cln_g, cln_b, qn_g, kn_g,
                lam_q1, lam_k1, lam_q2, lam_k2, subln_g, w_out, *, batch, seq):
    conv_ch = conv_w.shape[1]
    proj = _rms_matmul(x, norm_g, w_in.astype(BF16), tm=1024, tn=1024)
    mix_a = _conv_mixer(proj, conv_w.astype(F32), conv_b.astype(F32), cln_g.astype(F32),
                        cln_b.astype(F32), seq=seq, tt=256)
    qn, kn = _qkv_prep(proj, qn_g, kn_g, seq=seq, tt=512, col0=3 * conv_ch)
    lam_init = 0.8 - 0.6 * math.exp(-0.3 * layer_idx)
    lam = (jnp.exp(jnp.sum(lam_q1.astype(F32) * lam_k1.astype(F32)))
           - jnp.exp(jnp.sum(lam_q2.astype(F32) * lam_k2.astype(F32))) + lam_init)
    qkv_w = qn.shape[1]
    mix_b = _diff_attention(lam, qn, kn, proj, subln_g, batch=batch, seq=seq, tq=512,
                            v_col0=3 * conv_ch + 2 * qkv_w, gate_col0=3 * conv_ch + 3 * qkv_w,
                            post_scale=1.0 - lam_init)
    return _proj_residual(mix_a, 0, mix_b, 0, w_out.astype(BF16), x, tm=1024, tn=1024)


def _odd_layer(x, norm_g, w_in, a_re, a_im, log_dt, b_re, b_im, c_re, c_im, d_skip,
               w_glu, b_glu, w_out, *, batch, seq):
    e = w_glu.shape[0]
    proj = _rms_matmul(x, norm_g, w_in.astype(BF16), tm=1024, tn=1024)
    tables = _s5_tables(a_re, a_im, log_dt, b_re, b_im, c_re, c_im, S5_CHUNK)
    ys = _s5(proj, tables, batch=batch, seq=seq)
    out = _glu(ys, proj, d_skip, w_glu.astype(BF16), b_glu, tm=512)
    return _proj_residual(out, 0, out, 1, w_out.astype(BF16), x, tm=1024, tn=1024)


def kernel(x, e_norm_g, e_w_in, e_conv_w, e_conv_b, e_cln_g, e_cln_b, e_qn_g, e_kn_g, e_lam_q1, e_lam_k1, e_lam_q2, e_lam_k2, e_subln_g, e_w_out, o_norm_g, o_w_in, o_A_re, o_A_im, o_log_dt, o_B_re, o_B_im, o_C_re, o_C_im, o_D, o_w_glu, o_b_glu, o_w_out):
    batch, seq, d_model = x.shape
    depth = e_norm_g.shape[0] + o_norm_g.shape[0]
    h = x.reshape(batch * seq, d_model)
    for layer in range(depth):
        j = layer // 2
        if layer % 2 == 0:
            h = _even_layer(h, layer, e_norm_g[j], e_w_in[j], e_conv_w[j], e_conv_b[j], e_cln_g[j],
                            e_cln_b[j], e_qn_g[j], e_kn_g[j], e_lam_q1[j], e_lam_k1[j], e_lam_q2[j],
                            e_lam_k2[j], e_subln_g[j], e_w_out[j], batch=batch, seq=seq)
        else:
            h = _odd_layer(h, o_norm_g[j], o_w_in[j], o_A_re[j], o_A_im[j], o_log_dt[j], o_B_re[j],
                           o_B_im[j], o_C_re[j], o_C_im[j], o_D[j], o_w_glu[j], o_b_glu[j], o_w_out[j],
                           batch=batch, seq=seq)
    return h.reshape(batch, seq, d_model)
```

```python
import functools
import math

import jax
import jax.numpy as jnp
from jax import lax
from jax.experimental import pallas as pl
from jax.experimental.pallas import tpu as pltpu

F32 = jnp.float32
BF16 = jnp.bfloat16

EPS = 1e-6
ROPE_THETA = 10000.0
CONV_W = 31
DA_HEADS = 8
DA_HEAD_DIM = 64
S5_GROUP = 16
S5_STATE = 64

LANES = 128
CONV_HALO = 32
CONV_ROWS = 32
ATT_ROWS = 64
S5_CHUNK = 32
VMEM_LIMIT = 56 * 1024 * 1024


def _params(*sem):
    return pltpu.CompilerParams(dimension_semantics=sem, vmem_limit_bytes=VMEM_LIMIT)


def _silu(x):
    return x * jax.nn.sigmoid(x)


def _gelu_tanh(x):
    return 0.5 * x * (1.0 + jnp.tanh(math.sqrt(2.0 / math.pi) * (x + 0.044715 * (x * x * x))))


def _rms_matmul_kernel(x_ref, g_ref, w_ref, o_ref, h_ref):
    @pl.when(pl.program_id(1) == 0)
    def _():
        x = x_ref[...]
        ms = jnp.mean(x * x, axis=-1, keepdims=True)
        h_ref[...] = (x * lax.rsqrt(ms + EPS) * g_ref[...]).astype(h_ref.dtype)

    o_ref[...] = jnp.dot(h_ref[...], w_ref[...], preferred_element_type=F32).astype(o_ref.dtype)


def _rms_matmul(x, g, w, *, tm, tn):
    m, d = x.shape
    n = w.shape[1]
    return pl.pallas_call(
        _rms_matmul_kernel,
        grid=(m // tm, n // tn),
        in_specs=[
            pl.BlockSpec((tm, d), lambda i, j: (i, 0)),
            pl.BlockSpec((1, d), lambda i, j: (0, 0)),
            pl.BlockSpec((d, tn), lambda i, j: (0, j)),
        ],
        out_specs=pl.BlockSpec((tm, tn), lambda i, j: (i, j)),
        out_shape=jax.ShapeDtypeStruct((m, n), BF16),
        scratch_shapes=[pltpu.VMEM((tm, d), BF16)],
        compiler_params=_params("parallel", "arbitrary"),
        name="rms_matmul",
    )(x, g.reshape(1, d), w)


def _conv_kernel(val_ref, glu_ref, gate_ref, hval_ref, hglu_ref, w_ref, cb_ref, lg_ref, lb_ref,
                 o_ref, uext_ref, *, tt, tiles_per_seq):
    c = o_ref.shape[-1]
    nlt = c // LANES
    first = (pl.program_id(0) % tiles_per_seq) == 0
    f32 = lambda ref, idx=Ellipsis: ref[idx].astype(F32)
    halo = jnp.where(first, 0.0, f32(hval_ref) * jax.nn.sigmoid(f32(hglu_ref)))
    cur = f32(val_ref) * jax.nn.sigmoid(f32(glu_ref))
    for j in range(nlt):
        sl = slice(j * LANES, (j + 1) * LANES)
        uext_ref[j, pl.ds(0, CONV_HALO, stride=2), :] = halo[:, sl]
        uext_ref[j, pl.ds(2 * CONV_HALO, tt, stride=2), :] = cur[:, sl]
    lg = lg_ref[...]
    lb = lb_ref[...]

    def chunk(r, carry):
        base = pl.multiple_of(r * CONV_ROWS, CONV_ROWS)
        accs = []
        for j in range(nlt):
            sl = slice(j * LANES, (j + 1) * LANES)
            acc = jnp.broadcast_to(cb_ref[:, sl], (CONV_ROWS, LANES))
            for k in range(CONV_W):
                row = base + (CONV_HALO - CONV_W + 1 + k)
                acc = acc + w_ref[k:k + 1, sl] * uext_ref[j, pl.ds(2 * row, CONV_ROWS, stride=2), :]
            accs.append(acc)
        acc = jnp.concatenate(accs, axis=1)
        mean = jnp.mean(acc, axis=-1, keepdims=True)
        xc = acc - mean
        var = jnp.mean(xc * xc, axis=-1, keepdims=True)
        y = _silu(xc * lax.rsqrt(var + EPS) * lg + lb)
        gate = gate_ref[pl.ds(base, CONV_ROWS), :].astype(F32)
        o_ref[pl.ds(base, CONV_ROWS), :] = (y * _silu(gate)).astype(o_ref.dtype)
        return carry

    lax.fori_loop(0, tt // CONV_ROWS, chunk, 0, unroll=2)


def _conv_mixer(proj, conv_w, conv_b, ln_g, ln_b, *, seq, tt):
    t = proj.shape[0]
    c = conv_w.shape[1]
    hb = tt // CONV_HALO
    row = lambda a: a.reshape(1, c)
    return pl.pallas_call(
        functools.partial(_conv_kernel, tt=tt, tiles_per_seq=seq // tt),
        grid=(t // tt,),
        in_specs=[
            pl.BlockSpec((tt, c), lambda i: (i, 0)),
            pl.BlockSpec((tt, c), lambda i: (i, 1)),
            pl.BlockSpec((tt, c), lambda i: (i, 2)),
            pl.BlockSpec((CONV_HALO, c), lambda i: (jnp.maximum(i * hb - 1, 0), 0)),
            pl.BlockSpec((CONV_HALO, c), lambda i: (jnp.maximum(i * hb - 1, 0), 1)),
            pl.BlockSpec((CONV_W, c), lambda i: (0, 0)),
            pl.BlockSpec((1, c), lambda i: (0, 0)),
            pl.BlockSpec((1, c), lambda i: (0, 0)),
            pl.BlockSpec((1, c), lambda i: (0, 0)),
        ],
        out_specs=pl.BlockSpec((tt, c), lambda i: (i, 0)),
        out_shape=jax.ShapeDtypeStruct((t, c), BF16),
        scratch_shapes=[pltpu.VMEM((c // LANES, 2 * (CONV_HALO + tt), LANES), F32)],
        compiler_params=_params("parallel"),
        name="conv_mixer",
    )(proj, proj, proj, proj, proj, conv_w, row(conv_b), row(ln_g), row(ln_b))


def _qkv_kernel(q_ref, k_ref, cos_ref, sin_ref, qg_ref, kg_ref, ones_ref, qo_ref, ko_ref, *, q_scale):
    cos = cos_ref[...]
    sin = sin_ref[...]
    ones = ones_ref[...]
    lane = lax.broadcasted_iota(jnp.int32, cos.shape, 1)
    first_half = (lane % DA_HEAD_DIM) < (DA_HEAD_DIM // 2)

    def prep(x, g, scale):
        x2 = x * x
        hi = x2.astype(BF16)
        lo = (x2 - hi.astype(F32)).astype(BF16)
        ss = (jnp.dot(hi, ones, preferred_element_type=F32)
              + jnp.dot(lo, ones, preferred_element_type=F32))
        y = x * lax.rsqrt(ss * (1.0 / DA_HEAD_DIM) + EPS) * g
        partner = jnp.where(first_half, pltpu.roll(y, LANES - DA_HEAD_DIM // 2, 1),
                            pltpu.roll(y, DA_HEAD_DIM // 2, 1))
        return (y * cos + partner * sin) * scale

    for h in range(q_ref.shape[-1] // LANES):
        sl = slice(h * LANES, (h + 1) * LANES)
        qo_ref[:, sl] = prep(q_ref[:, sl].astype(F32), qg_ref[...], q_scale).astype(qo_ref.dtype)
        ko_ref[:, sl] = prep(k_ref[:, sl].astype(F32), kg_ref[...], 1.0).astype(ko_ref.dtype)


def _qkv_prep(proj, qn_g, kn_g, *, seq, tt, col0):
    t = proj.shape[0]
    w = DA_HEADS * 2 * DA_HEAD_DIM
    half = DA_HEAD_DIM // 2
    freqs = ROPE_THETA ** (-jnp.arange(half, dtype=F32) / half)
    ang = jnp.arange(seq, dtype=F32)[:, None] * freqs[None, :]
    cos = jnp.tile(jnp.cos(ang), (1, LANES // half))
    sin = jnp.tile(jnp.concatenate([-jnp.sin(ang), jnp.sin(ang)], axis=-1), (1, LANES // DA_HEAD_DIM))
    blk = jnp.arange(LANES) // DA_HEAD_DIM
    ones = (blk[:, None] == blk[None, :]).astype(BF16)
    tile_g = lambda g: jnp.tile(g.astype(F32), LANES // DA_HEAD_DIM).reshape(1, LANES)
    cb = col0 // w
    nseq = seq // tt
    out = jax.ShapeDtypeStruct((t, w), BF16)
    return pl.pallas_call(
        functools.partial(_qkv_kernel, q_scale=math.log2(math.e) * DA_HEAD_DIM ** -0.5),
        grid=(t // tt,),
        in_specs=[
            pl.BlockSpec((tt, w), lambda i: (i, cb)),
            pl.BlockSpec((tt, w), lambda i: (i, cb + 1)),
            pl.BlockSpec((tt, LANES), lambda i: (i % nseq, 0)),
            pl.BlockSpec((tt, LANES), lambda i: (i % nseq, 0)),
            pl.BlockSpec((1, LANES), lambda i: (0, 0)),
            pl.BlockSpec((1, LANES), lambda i: (0, 0)),
            pl.BlockSpec((LANES, LANES), lambda i: (0, 0)),
        ],
        out_specs=[pl.BlockSpec((tt, w), lambda i: (i, 0))] * 2,
        out_shape=[out, out],
        compiler_params=_params("parallel"),
        name="qkv_prep",
    )(proj, proj, cos, sin, tile_g(qn_g), tile_g(kn_g), ones)


def _attn_kernel(lam_ref, q_ref, k_ref, v_ref, bg_ref, sg_ref, o_ref,
                 qs_ref, vaug_ref, s0_ref, s1_ref, s2_ref, p0_ref, p1_ref, p2_ref, m_ref,
                 a0_ref, a1_ref, a2_ref, acc_ref, *, tq, post_scale):
    i = pl.program_id(2)
    s_refs, p_refs, a_refs = (s0_ref, s1_ref, s2_ref), (p0_ref, p1_ref, p2_ref), (a0_ref, a1_ref, a2_ref)

    @pl.when(i == 0)
    def _():
        vaug_ref[:, 0:LANES] = v_ref[...]
        vaug_ref[:, LANES:2 * LANES] = jnp.ones(v_ref.shape, vaug_ref.dtype)

    q = q_ref[...]
    lane = lax.broadcasted_iota(jnp.int32, q.shape, 1)
    zero = jnp.zeros_like(q)
    qs_ref[0:tq, :] = jnp.where(lane < DA_HEAD_DIM, q, zero)
    qs_ref[tq:2 * tq, :] = jnp.where(lane >= DA_HEAD_DIM, q, zero)
    m_ref[...] = jnp.full(m_ref.shape, -jnp.inf, F32)
    acc_ref[...] = jnp.zeros(acc_ref.shape, F32)
    reps = tq // LANES

    def scores(j, slot):
        start = pl.multiple_of(j * tq, tq)
        s_refs[slot][...] = lax.dot_general(qs_ref[...], k_ref[pl.ds(start, tq), :],
                                            (((1,), (1,)), ((), ())), preferred_element_type=F32)

    def softmax(slot, masked):
        s_ref, p_ref, a_ref = s_refs[slot], p_refs[slot], a_refs[slot]
        for r0 in range(0, 2 * tq, ATT_ROWS):
            rs = slice(r0, r0 + ATT_ROWS)
            s = s_ref[rs, :]
            if masked:
                row = (r0 % tq) + lax.broadcasted_iota(jnp.int32, s.shape, 0)
                col = lax.broadcasted_iota(jnp.int32, s.shape, 1)
                s = jnp.where(col <= row, s, -jnp.inf)
            m_prev = m_ref[rs, :]
            m_new = jnp.maximum(m_prev, jnp.max(s, axis=-1, keepdims=True))
            a_ref[rs, :] = jnp.exp2(m_prev - m_new)
            m_ref[rs, :] = m_new
            p_ref[rs, :] = jnp.exp2(s - jnp.concatenate([m_new] * reps, axis=1)).astype(p_ref.dtype)

    def accumulate(j, slot):
        start = pl.multiple_of(j * tq, tq)
        alpha = a_refs[slot][...]
        acc_ref[...] = jnp.concatenate([alpha, alpha], axis=1) * acc_ref[...] + jnp.dot(
            p_refs[slot][...], vaug_ref[pl.ds(start, tq), :], preferred_element_type=F32)

    scores(0, 0)

    @pl.when(i == 0)
    def _():
        softmax(0, True)
        accumulate(0, 0)

    @pl.when(i > 0)
    def _():
        scores(1, 1)
        softmax(0, False)
        npairs = (i - 1) // 2

        def pair(tt, carry):
            t = 2 * tt
            scores(t + 2, 0)
            softmax(1, False)
            accumulate(t, 0)
            scores(t + 3, 1)
            softmax(0, False)
            accumulate(t + 1, 1)
            return carry

        lax.fori_loop(0, npairs, pair, 0)

        @pl.when(i % 2 == 1)
        def _():
            softmax(1, True)
            accumulate(i - 1, 0)
            accumulate(i, 1)

        @pl.when(i % 2 == 0)
        def _():
            scores(i, 2)
            softmax(1, False)
            accumulate(i - 2, 0)
            softmax(2, True)
            accumulate(i - 1, 1)
            accumulate(i, 2)

    o = acc_ref[:, 0:LANES] / acc_ref[:, LANES:2 * LANES]
    d = o[:tq] - lam_ref[0] * o[tq:]
    ms = jnp.mean(d * d, axis=-1, keepdims=True)
    y = d * lax.rsqrt(ms + EPS) * sg_ref[...] * post_scale
    o_ref[...] = (y * _silu(bg_ref[...].astype(F32))).astype(o_ref.dtype)


def _diff_attention(lam, qn, kn, proj, subln_g, *, batch, seq, tq, v_col0, gate_col0, post_scale):
    t, w = qn.shape
    nq = seq // tq
    vcb = v_col0 // LANES
    gcb = gate_col0 // LANES
    stat = pltpu.VMEM((2 * tq, LANES), F32)
    sbuf = pltpu.VMEM((2 * tq, tq), F32)
    pbuf = pltpu.VMEM((2 * tq, tq), BF16)
    return pl.pallas_call(
        functools.partial(_attn_kernel, tq=tq, post_scale=post_scale),
        grid=(batch, DA_HEADS, nq),
        in_specs=[
            pl.BlockSpec(memory_space=pltpu.SMEM),
            pl.BlockSpec((tq, LANES), lambda b, h, i: (b * nq + i, h)),
            pl.BlockSpec((seq, LANES), lambda b, h, i: (b, h)),
            pl.BlockSpec((seq, LANES), lambda b, h, i: (b, vcb + h)),
            pl.BlockSpec((tq, LANES), lambda b, h, i: (b * nq + i, gcb + h)),
            pl.BlockSpec((1, LANES), lambda b, h, i: (0, 0)),
        ],
        out_specs=pl.BlockSpec((tq, LANES), lambda b, h, i: (b * nq + i, h)),
        out_shape=jax.ShapeDtypeStruct((t, w), BF16),
        scratch_shapes=[pltpu.VMEM((2 * tq, LANES), BF16), pltpu.VMEM((seq, 2 * LANES), BF16),
                        sbuf, sbuf, sbuf, pbuf, pbuf, pbuf, stat, stat, stat, stat,
                        pltpu.VMEM((2 * tq, 2 * LANES), F32)],
        compiler_params=_params("parallel", "parallel", "arbitrary"),
        name="diff_attention",
    )(lam.reshape(1), qn, kn, proj, proj, subln_g.astype(F32).reshape(1, LANES))


def _proj_res_kernel(a_ref, b_ref, wa_ref, wb_ref, x_ref, o_ref):
    y = (jnp.dot(a_ref[...], wa_ref[...], preferred_element_type=F32)
         + jnp.dot(b_ref[...], wb_ref[...], preferred_element_type=F32))
    o_ref[...] = x_ref[...] + y


def _proj_residual(a, a_cb, b, b_cb, w, x, *, tm, tn):
    m, n = x.shape
    kh = w.shape[0] // 2
    return pl.pallas_call(
        _proj_res_kernel,
        grid=(m // tm, n // tn),
        in_specs=[
            pl.BlockSpec((tm, kh), lambda i, j: (i, a_cb)),
            pl.BlockSpec((tm, kh), lambda i, j: (i, b_cb)),
            pl.BlockSpec((kh, tn), lambda i, j: (0, j)),
            pl.BlockSpec((kh, tn), lambda i, j: (1, j)),
            pl.BlockSpec((tm, tn), lambda i, j: (i, j)),
        ],
        out_specs=pl.BlockSpec((tm, tn), lambda i, j: (i, j)),
        out_shape=jax.ShapeDtypeStruct((m, n), F32),
        compiler_params=_params("parallel", "arbitrary"),
        name="proj_residual",
    )(a, b, w, w, x)


def _split_bf16(x):
    hi = x.astype(BF16)
    return hi, (x - hi.astype(F32)).astype(BF16)


def _s5_kernel(u_ref, d_ref, sw_ref, cc_ref, cw_ref, lr_ref, li_ref, y_ref, ut_ref, yt_ref, toep_ref,
               *, chunks_per_seq):
    p = S5_STATE
    gb, width, ncol = ut_ref.shape
    mm = cc_ref.shape[1]
    nl = width // mm
    nc = chunks_per_seq
    reps = ncol // LANES
    lane = lax.broadcasted_iota(jnp.int32, (p, ncol), 1)
    cpos = lane % nc

    def shifted(x, k):
        return jnp.where(cpos >= k, pltpu.roll(x, k, 1), 0.0)

    for b in range(ncol // nc):
        for l in range(nl):
            xt = u_ref[(b * nl + l) * nc:(b * nl + l + 1) * nc, :].astype(F32).T.astype(ut_ref.dtype)
            for g in range(gb):
                ut_ref[g, l * mm:(l + 1) * mm, b * nc:(b + 1) * nc] = xt[g * mm:(g + 1) * mm, :]

    for g in range(gb):
        u = ut_ref[g]
        sw_hi, sw_lo = _split_bf16(sw_ref[g])
        c_hi, c_lo = _split_bf16(cc_ref[g])
        kcat = (jnp.dot(c_hi, sw_hi, preferred_element_type=F32)
                + jnp.dot(c_hi, sw_lo, preferred_element_type=F32)
                + jnp.dot(c_lo, sw_hi, preferred_element_type=F32))
        z = jnp.concatenate([kcat, jnp.zeros_like(kcat)], axis=1)
        rolled = [z if r == 0 else pltpu.roll(z, 2 * width - r, 1) for r in range(0, LANES, mm)]
        for l in range(nl):
            off = (nl - 1 - l) * mm
            base = off - off % LANES
            strip = rolled[(off % LANES) // mm][:, base:base + width]
            toep_ref[l * mm:(l + 1) * mm, :] = strip.astype(toep_ref.dtype)
        s = jnp.dot(sw_hi, u, preferred_element_type=F32)
        xr, xi = s[:p], s[p:]
        lr = jnp.concatenate([lr_ref[g]] * reps, axis=1)
        li = jnp.concatenate([li_ref[g]] * reps, axis=1)
        k = 1
        while k < chunks_per_seq:
            sr, si = shifted(xr, k), shifted(xi, k)
            xr, xi = xr + (lr * sr - li * si), xi + (lr * si + li * sr)
            lr, li = lr * lr - li * li, 2.0 * (lr * li)
            k *= 2
        h = jnp.concatenate([shifted(xr, 1), shifted(xi, 1)], axis=0).astype(BF16)
        yt_ref[g] = (jnp.dot(toep_ref[...], u, preferred_element_type=F32)
                     + jnp.dot(cw_ref[g], h, preferred_element_type=F32))

    for b in range(ncol // nc):
        for l in range(nl):
            rows = slice((b * nl + l) * nc, (b * nl + l + 1) * nc)
            zt = jnp.concatenate([yt_ref[g, l * mm:(l + 1) * mm, b * nc:(b + 1) * nc] for g in range(gb)], axis=0)
            y = zt.T + d_ref[...] * u_ref[rows, :].astype(F32)
            y_ref[rows, :] = _gelu_tanh(y).astype(y_ref.dtype)


def _s5_tables(a_re, a_im, log_dt, b_re, b_im, c_re, c_im, chunk):
    g, p, m = b_re.shape
    dt = jnp.exp(log_dt.astype(F32))[:, None]
    a_re, a_im = a_re.astype(F32), a_im.astype(F32)
    mag = jnp.exp(a_re * dt)
    lb_re, lb_im = mag * jnp.cos(a_im * dt), mag * jnp.sin(a_im * dt)
    den = a_re * a_re + a_im * a_im
    nr, ni = lb_re - 1.0, lb_im
    fr = (nr * a_re + ni * a_im) / den
    fi = (ni * a_re - nr * a_im) / den
    bb_re = fr[..., None] * b_re - fi[..., None] * b_im
    bb_im = fr[..., None] * b_im + fi[..., None] * b_re
    d = jnp.arange(chunk + 1, dtype=F32)[None, :, None]
    pmag = jnp.exp((a_re * dt)[:, None, :] * d)
    pang = (a_im * dt)[:, None, :] * d
    pr, pi = pmag * jnp.cos(pang), pmag * jnp.sin(pang)
    c_re, c_im = c_re.astype(F32), c_im.astype(F32)
    hp = lax.Precision.HIGHEST
    rep_l = jnp.repeat(jnp.eye(chunk, dtype=F32)[::-1], m, axis=1)
    til_m = jnp.tile(jnp.eye(m, dtype=F32), (1, chunk))
    prt = jnp.einsum('gdp,dj->gpj', pr[:, :chunk], rep_l, precision=hp)
    pit = jnp.einsum('gdp,dj->gpj', pi[:, :chunk], rep_l, precision=hp)
    bbr = jnp.einsum('gpm,mj->gpj', bb_re, til_m, precision=hp)
    bbi = jnp.einsum('gpm,mj->gpj', bb_im, til_m, precision=hp)
    sw = jnp.concatenate([prt * bbr - pit * bbi, prt * bbi + pit * bbr], axis=1)
    cc = jnp.concatenate([c_re, -c_im], axis=-1)
    pr1, pi1 = pr[:, 1:, None, :], pi[:, 1:, None, :]
    cr, ci = c_re[:, None], c_im[:, None]
    cw = jnp.concatenate([cr * pr1 - ci * pi1, -(cr * pi1 + ci * pr1)], axis=-1)
    cw = cw.reshape(g, chunk * m, 2 * p)
    lam_r = jnp.broadcast_to(pr[:, chunk, :, None], (g, p, LANES))
    lam_i = jnp.broadcast_to(pi[:, chunk, :, None], (g, p, LANES))
    return sw, cc, cw.astype(BF16), lam_r, lam_i


def _s5(proj, d_skip, tables, *, batch, seq):
    sw, cc, cw, lam_r, lam_i = tables
    t = proj.shape[0]
    g, m = cc.shape[0], cc.shape[1]
    e = g * m
    chunk = sw.shape[2] // m
    nc = seq // chunk
    ncol = t // chunk
    gb = LANES // m
    up = proj.reshape(batch, nc, chunk, proj.shape[1])[..., :e].transpose(0, 2, 1, 3).reshape(t, e)
    wspec = lambda a: pl.BlockSpec((gb,) + a.shape[1:], lambda i: (i, 0, 0))
    yp = pl.pallas_call(
        functools.partial(_s5_kernel, chunks_per_seq=nc),
        grid=(g // gb,),
        in_specs=[pl.BlockSpec((t, LANES), lambda i: (0, i)), pl.BlockSpec((1, LANES), lambda i: (0, i)),
                  wspec(sw), wspec(cc), wspec(cw), wspec(lam_r), wspec(lam_i)],
        out_specs=pl.BlockSpec((t, LANES), lambda i: (0, i)),
        out_shape=jax.ShapeDtypeStruct((t, e), BF16),
        scratch_shapes=[pltpu.VMEM((gb, chunk * m, ncol), BF16), pltpu.VMEM((gb, chunk * m, ncol), F32),
                        pltpu.VMEM((chunk * m, chunk * m), BF16)],
        compiler_params=_params("parallel"),
        name="s5_chunked",
    )(up, d_skip.astype(F32).reshape(1, e), sw, cc, cw, lam_r, lam_i)
    return yp.reshape(batch, chunk, nc, e).transpose(0, 2, 1, 3).reshape(t, e)


def _glu_out_kernel(z_ref, gate_ref, wg_ref, b_ref, wo_ref, x_ref, o_ref):
    half = o_ref.shape[0] // 2
    for r0 in (0, half):
        rs = slice(r0, r0 + half)
        z = z_ref[rs, :]
        t = jnp.dot(z, wg_ref[...], preferred_element_type=F32) + b_ref[...]
        out = z.astype(F32) * jax.nn.sigmoid(t) * _silu(gate_ref[rs, :].astype(F32))
        o_ref[rs, :] = x_ref[rs, :] + jnp.dot(out.astype(BF16), wo_ref[...], preferred_element_type=F32)


def _glu_out(z, proj, w_glu, b_glu, w_out, x, *, tm):
    t, e = z.shape
    n = w_out.shape[1]
    resident = lambda a: pl.BlockSpec(a.shape, lambda i: (0, 0), pipeline_mode=pl.Buffered(1))
    return pl.pallas_call(
        _glu_out_kernel,
        grid=(t // tm,),
        in_specs=[
            pl.BlockSpec((tm, e), lambda i: (i, 0)),
            pl.BlockSpec((tm, e), lambda i: (i, 1)),
            resident(w_glu),
            pl.BlockSpec((1, e), lambda i: (0, 0)),
            resident(w_out),
            pl.BlockSpec((tm, n), lambda i: (i, 0)),
        ],
        out_specs=pl.BlockSpec((tm, n), lambda i: (i, 0)),
        out_shape=jax.ShapeDtypeStruct((t, n), F32),
        compiler_params=_params("parallel"),
        name="glu_out_residual",
    )(z, proj, w_glu, b_glu.astype(F32).reshape(1, e), w_out, x)


def _even_layer(x, layer_idx, norm_g, w_in, conv_w, conv_b, cln_g, cln_b, qn_g, kn_g,
                lam_q1, lam_k1, lam_q2, lam_k2, subln_g, w_out, *, batch, seq):
    conv_ch = conv_w.shape[1]
    proj = _rms_matmul(x, norm_g, w_in.astype(BF16), tm=1024, tn=1024)
    mix_a = _conv_mixer(proj, conv_w.astype(F32), conv_b.astype(F32), cln_g.astype(F32),
                        cln_b.astype(F32), seq=seq, tt=256)
    qn, kn = _qkv_prep(proj, qn_g, kn_g, seq=seq, tt=512, col0=3 * conv_ch)
    lam_init = 0.8 - 0.6 * math.exp(-0.3 * layer_idx)
    lam = (jnp.exp(jnp.sum(lam_q1.astype(F32) * lam_k1.astype(F32)))
           - jnp.exp(jnp.sum(lam_q2.astype(F32) * lam_k2.astype(F32))) + lam_init)
    qkv_w = qn.shape[1]
    mix_b = _diff_attention(lam, qn, kn, proj, subln_g, batch=batch, seq=seq, tq=512,
                            v_col0=3 * conv_ch + 2 * qkv_w, gate_col0=3 * conv_ch + 3 * qkv_w,
                            post_scale=1.0 - lam_init)
    return _proj_residual(mix_a, 0, mix_b, 0, w_out.astype(BF16), x, tm=1024, tn=1024)


def _odd_layer(x, norm_g, w_in, a_re, a_im, log_dt, b_re, b_im, c_re, c_im, d_skip,
               w_glu, b_glu, w_out, *, batch, seq):
    proj = _rms_matmul(x, norm_g, w_in.astype(BF16), tm=1024, tn=1024)
    tables = _s5_tables(a_re, a_im, log_dt, b_re, b_im, c_re, c_im, S5_CHUNK)
    z = _s5(proj, d_skip, tables, batch=batch, seq=seq)
    return _glu_out(z, proj, w_glu.astype(BF16), b_glu, w_out.astype(BF16), x, tm=512)


def kernel(x, e_norm_g, e_w_in, e_conv_w, e_conv_b, e_cln_g, e_cln_b, e_qn_g, e_kn_g, e_lam_q1, e_lam_k1, e_lam_q2, e_lam_k2, e_subln_g, e_w_out, o_norm_g, o_w_in, o_A_re, o_A_im, o_log_dt, o_B_re, o_B_im, o_C_re, o_C_im, o_D, o_w_glu, o_b_glu, o_w_out):
    batch, seq, d_model = x.shape
    depth = e_norm_g.shape[0] + o_norm_g.shape[0]
    h = x.reshape(batch * seq, d_model)
    for layer in range(depth):
        j = layer // 2
        if layer % 2 == 0:
            h = _even_layer(h, layer, e_norm_g[j], e_w_in[j], e_conv_w[j], e_conv_b[j], e_cln_g[j],
                            e_cln_b[j], e_qn_g[j], e_kn_g[j], e_lam_q1[j], e_lam_k1[j], e_lam_q2[j],
                            e_lam_k2[j], e_subln_g[j], e_w_out[j], batch=batch, seq=seq)
        else:
            h = _odd_layer(h, o_norm_g[j], o_w_in[j], o_A_re[j], o_A_im[j], o_log_dt[j], o_B_re[j],
                           o_B_im[j], o_C_re[j], o_C_im[j], o_D[j], o_w_glu[j], o_b_glu[j], o_w_out[j],
                           batch=batch, seq=seq)
    return h.reshape(batch, seq, d_model)
```

```python
import functools
import math

import jax
import jax.numpy as jnp
from jax import lax
from jax.experimental import pallas as pl
from jax.experimental.pallas import tpu as pltpu

F32 = jnp.float32
BF16 = jnp.bfloat16

EPS = 1e-6
ROPE_THETA = 10000.0
CONV_W = 31
DA_HEADS = 8
DA_HEAD_DIM = 64
S5_GROUP = 16
S5_STATE = 64

LANES = 128
CONV_HALO = 32
CONV_ROWS = 32
ATT_ROWS = 64
S5_CHUNK = 32
VMEM_LIMIT = 56 * 1024 * 1024


def _params(*sem):
    return pltpu.CompilerParams(dimension_semantics=sem, vmem_limit_bytes=VMEM_LIMIT)


def _silu(x):
    return x * jax.nn.sigmoid(x)


def _gelu_tanh(x):
    return 0.5 * x * (1.0 + jnp.tanh(math.sqrt(2.0 / math.pi) * (x + 0.044715 * (x * x * x))))


def _rms_matmul_kernel(x_ref, g_ref, w_ref, o_ref):
    half = o_ref.shape[0] // 2
    for r0 in (0, half):
        rs = slice(r0, r0 + half)
        x = x_ref[rs, :]
        ms = jnp.mean(x * x, axis=-1, keepdims=True)
        h = (x * lax.rsqrt(ms + EPS) * g_ref[...]).astype(BF16)
        o_ref[rs, :] = jnp.dot(h, w_ref[...], preferred_element_type=F32).astype(o_ref.dtype)


def _rms_matmul(x, g, w, *, tm, nsplit):
    m, d = x.shape
    n = w.shape[1]
    tn = n // nsplit
    return pl.pallas_call(
        _rms_matmul_kernel,
        grid=(nsplit, m // tm),
        in_specs=[
            pl.BlockSpec((tm, d), lambda j, i: (i, 0)),
            pl.BlockSpec((1, d), lambda j, i: (0, 0)),
            pl.BlockSpec((d, tn), lambda j, i: (0, j), pipeline_mode=pl.Buffered(1)),
        ],
        out_specs=pl.BlockSpec((tm, tn), lambda j, i: (i, j)),
        out_shape=jax.ShapeDtypeStruct((m, n), BF16),
        compiler_params=_params("parallel", "parallel"),
        name="rms_matmul",
    )(x, g.reshape(1, d), w)


def _conv_kernel(val_ref, glu_ref, gate_ref, hval_ref, hglu_ref, w_ref, cb_ref, lg_ref, lb_ref,
                 o_ref, uext_ref, *, tt, tiles_per_seq):
    c = o_ref.shape[-1]
    nlt = c // LANES
    first = (pl.program_id(0) % tiles_per_seq) == 0
    f32 = lambda ref, idx=Ellipsis: ref[idx].astype(F32)
    halo = jnp.where(first, 0.0, f32(hval_ref) * jax.nn.sigmoid(f32(hglu_ref)))
    cur = f32(val_ref) * jax.nn.sigmoid(f32(glu_ref))
    for j in range(nlt):
        sl = slice(j * LANES, (j + 1) * LANES)
        uext_ref[j, pl.ds(0, CONV_HALO, stride=2), :] = halo[:, sl]
        uext_ref[j, pl.ds(2 * CONV_HALO, tt, stride=2), :] = cur[:, sl]
    lg = lg_ref[...]
    lb = lb_ref[...]

    def chunk(r, carry):
        base = pl.multiple_of(r * CONV_ROWS, CONV_ROWS)
        accs = []
        for j in range(nlt):
            sl = slice(j * LANES, (j + 1) * LANES)
            acc = jnp.broadcast_to(cb_ref[:, sl], (CONV_ROWS, LANES))
            for k in range(CONV_W):
                row = base + (CONV_HALO - CONV_W + 1 + k)
                acc = acc + w_ref[k:k + 1, sl] * uext_ref[j, pl.ds(2 * row, CONV_ROWS, stride=2), :]
            accs.append(acc)
        acc = jnp.concatenate(accs, axis=1)
        mean = jnp.mean(acc, axis=-1, keepdims=True)
        xc = acc - mean
        var = jnp.mean(xc * xc, axis=-1, keepdims=True)
        y = _silu(xc * lax.rsqrt(var + EPS) * lg + lb)
        gate = gate_ref[pl.ds(base, CONV_ROWS), :].astype(F32)
        o_ref[pl.ds(base, CONV_ROWS), :] = (y * _silu(gate)).astype(o_ref.dtype)
        return carry

    lax.fori_loop(0, tt // CONV_ROWS, chunk, 0, unroll=2)


def _conv_mixer(proj, conv_w, conv_b, ln_g, ln_b, *, seq, tt):
    t = proj.shape[0]
    c = conv_w.shape[1]
    hb = tt // CONV_HALO
    row = lambda a: a.reshape(1, c)
    return pl.pallas_call(
        functools.partial(_conv_kernel, tt=tt, tiles_per_seq=seq // tt),
        grid=(t // tt,),
        in_specs=[
            pl.BlockSpec((tt, c), lambda i: (i, 0)),
            pl.BlockSpec((tt, c), lambda i: (i, 1)),
            pl.BlockSpec((tt, c), lambda i: (i, 2)),
            pl.BlockSpec((CONV_HALO, c), lambda i: (jnp.maximum(i * hb - 1, 0), 0)),
            pl.BlockSpec((CONV_HALO, c), lambda i: (jnp.maximum(i * hb - 1, 0), 1)),
            pl.BlockSpec((CONV_W, c), lambda i: (0, 0)),
            pl.BlockSpec((1, c), lambda i: (0, 0)),
            pl.BlockSpec((1, c), lambda i: (0, 0)),
            pl.BlockSpec((1, c), lambda i: (0, 0)),
        ],
        out_specs=pl.BlockSpec((tt, c), lambda i: (i, 0)),
        out_shape=jax.ShapeDtypeStruct((t, c), BF16),
        scratch_shapes=[pltpu.VMEM((c // LANES, 2 * (CONV_HALO + tt), LANES), F32)],
        compiler_params=_params("parallel"),
        name="conv_mixer",
    )(proj, proj, proj, proj, proj, conv_w, row(conv_b), row(ln_g), row(ln_b))


def _qkv_kernel(q_ref, k_ref, cos_ref, sin_ref, qg_ref, kg_ref, ones_ref, qo_ref, ko_ref, *, q_scale):
    cos = cos_ref[...]
    sin = sin_ref[...]
    ones = ones_ref[...]
    lane = lax.broadcasted_iota(jnp.int32, cos.shape, 1)
    first_half = (lane % DA_HEAD_DIM) < (DA_HEAD_DIM // 2)

    def prep(x, g, scale):
        x2 = x * x
        hi = x2.astype(BF16)
        lo = (x2 - hi.astype(F32)).astype(BF16)
        ss = (jnp.dot(hi, ones, preferred_element_type=F32)
              + jnp.dot(lo, ones, preferred_element_type=F32))
        y = x * lax.rsqrt(ss * (1.0 / DA_HEAD_DIM) + EPS) * g
        partner = jnp.where(first_half, pltpu.roll(y, LANES - DA_HEAD_DIM // 2, 1),
                            pltpu.roll(y, DA_HEAD_DIM // 2, 1))
        return (y * cos + partner * sin) * scale

    for h in range(q_ref.shape[-1] // LANES):
        sl = slice(h * LANES, (h + 1) * LANES)
        qo_ref[:, sl] = prep(q_ref[:, sl].astype(F32), qg_ref[...], q_scale).astype(qo_ref.dtype)
        ko_ref[:, sl] = prep(k_ref[:, sl].astype(F32), kg_ref[...], 1.0).astype(ko_ref.dtype)


def _qkv_prep(proj, qn_g, kn_g, *, seq, tt, col0):
    t = proj.shape[0]
    w = DA_HEADS * 2 * DA_HEAD_DIM
    half = DA_HEAD_DIM // 2
    freqs = ROPE_THETA ** (-jnp.arange(half, dtype=F32) / half)
    ang = jnp.arange(seq, dtype=F32)[:, None] * freqs[None, :]
    cos = jnp.tile(jnp.cos(ang), (1, LANES // half))
    sin = jnp.tile(jnp.concatenate([-jnp.sin(ang), jnp.sin(ang)], axis=-1), (1, LANES // DA_HEAD_DIM))
    blk = jnp.arange(LANES) // DA_HEAD_DIM
    ones = (blk[:, None] == blk[None, :]).astype(BF16)
    tile_g = lambda g: jnp.tile(g.astype(F32), LANES // DA_HEAD_DIM).reshape(1, LANES)
    cb = col0 // w
    nseq = seq // tt
    out = jax.ShapeDtypeStruct((t, w), BF16)
    return pl.pallas_call(
        functools.partial(_qkv_kernel, q_scale=math.log2(math.e) * DA_HEAD_DIM ** -0.5),
        grid=(t // tt,),
        in_specs=[
            pl.BlockSpec((tt, w), lambda i: (i, cb)),
            pl.BlockSpec((tt, w), lambda i: (i, cb + 1)),
            pl.BlockSpec((tt, LANES), lambda i: (i % nseq, 0)),
            pl.BlockSpec((tt, LANES), lambda i: (i % nseq, 0)),
            pl.BlockSpec((1, LANES), lambda i: (0, 0)),
            pl.BlockSpec((1, LANES), lambda i: (0, 0)),
            pl.BlockSpec((LANES, LANES), lambda i: (0, 0)),
        ],
        out_specs=[pl.BlockSpec((tt, w), lambda i: (i, 0))] * 2,
        out_shape=[out, out],
        compiler_params=_params("parallel"),
        name="qkv_prep",
    )(proj, proj, cos, sin, tile_g(qn_g), tile_g(kn_g), ones)


def _attn_kernel(lam_ref, q_ref, k_ref, v_ref, bg_ref, sg_ref, o_ref,
                 qs_ref, vaug_ref, s0_ref, s1_ref, s2_ref, p0_ref, p1_ref, p2_ref, m_ref,
                 a0_ref, a1_ref, a2_ref, acc_ref, *, tq, post_scale):
    i = pl.program_id(2)
    s_refs, p_refs, a_refs = (s0_ref, s1_ref, s2_ref), (p0_ref, p1_ref, p2_ref), (a0_ref, a1_ref, a2_ref)

    @pl.when(i == 0)
    def _():
        vaug_ref[:, 0:LANES] = v_ref[...]
        vaug_ref[:, LANES:2 * LANES] = jnp.ones(v_ref.shape, vaug_ref.dtype)

    q = q_ref[...]
    lane = lax.broadcasted_iota(jnp.int32, q.shape, 1)
    zero = jnp.zeros_like(q)
    qs_ref[0:tq, :] = jnp.where(lane < DA_HEAD_DIM, q, zero)
    qs_ref[tq:2 * tq, :] = jnp.where(lane >= DA_HEAD_DIM, q, zero)
    m_ref[...] = jnp.full(m_ref.shape, -jnp.inf, F32)
    acc_ref[...] = jnp.zeros(acc_ref.shape, F32)
    reps = tq // LANES

    def scores(j, slot):
        start = pl.multiple_of(j * tq, tq)
        s_refs[slot][...] = lax.dot_general(qs_ref[...], k_ref[pl.ds(start, tq), :],
                                            (((1,), (1,)), ((), ())), preferred_element_type=F32)

    def softmax(slot, masked):
        s_ref, p_ref, a_ref = s_refs[slot], p_refs[slot], a_refs[slot]
        for r0 in range(0, 2 * tq, ATT_ROWS):
            rs = slice(r0, r0 + ATT_ROWS)
            s = s_ref[rs, :]
            if masked:
                row = (r0 % tq) + lax.broadcasted_iota(jnp.int32, s.shape, 0)
                col = lax.broadcasted_iota(jnp.int32, s.shape, 1)
                s = jnp.where(col <= row, s, -jnp.inf)
            m_prev = m_ref[rs, :]
            m_new = jnp.maximum(m_prev, jnp.max(s, axis=-1, keepdims=True))
            a_ref[rs, :] = jnp.exp2(m_prev - m_new)
            m_ref[rs, :] = m_new
            p_ref[rs, :] = jnp.exp2(s - jnp.concatenate([m_new] * reps, axis=1)).astype(p_ref.dtype)

    def accumulate(j, slot):
        start = pl.multiple_of(j * tq, tq)
        alpha = a_refs[slot][...]
        acc_ref[...] = jnp.concatenate([alpha, alpha], axis=1) * acc_ref[...] + jnp.dot(
            p_refs[slot][...], vaug_ref[pl.ds(start, tq), :], preferred_element_type=F32)

    scores(0, 0)

    @pl.when(i == 0)
    def _():
        softmax(0, True)
        accumulate(0, 0)

    @pl.when(i > 0)
    def _():
        scores(1, 1)
        softmax(0, False)
        npairs = (i - 1) // 2

        def pair(tt, carry):
            t = 2 * tt
            scores(t + 2, 0)
            softmax(1, False)
            accumulate(t, 0)
            scores(t + 3, 1)
            softmax(0, False)
            accumulate(t + 1, 1)
            return carry

        lax.fori_loop(0, npairs, pair, 0)

        @pl.when(i % 2 == 1)
        def _():
            softmax(1, True)
            accumulate(i - 1, 0)
            accumulate(i, 1)

        @pl.when(i % 2 == 0)
        def _():
            scores(i, 2)
            softmax(1, False)
            accumulate(i - 2, 0)
            softmax(2, True)
            accumulate(i - 1, 1)
            accumulate(i, 2)

    o = acc_ref[:, 0:LANES] / acc_ref[:, LANES:2 * LANES]
    d = o[:tq] - lam_ref[0] * o[tq:]
    ms = jnp.mean(d * d, axis=-1, keepdims=True)
    y = d * lax.rsqrt(ms + EPS) * sg_ref[...] * post_scale
    o_ref[...] = (y * _silu(bg_ref[...].astype(F32))).astype(o_ref.dtype)


def _diff_attention(lam, qn, kn, proj, subln_g, *, batch, seq, tq, v_col0, gate_col0, post_scale):
    t, w = qn.shape
    nq = seq // tq
    vcb = v_col0 // LANES
    gcb = gate_col0 // LANES
    stat = pltpu.VMEM((2 * tq, LANES), F32)
    sbuf = pltpu.VMEM((2 * tq, tq), F32)
    pbuf = pltpu.VMEM((2 * tq, tq), BF16)
    return pl.pallas_call(
        functools.partial(_attn_kernel, tq=tq, post_scale=post_scale),
        grid=(batch, DA_HEADS, nq),
        in_specs=[
            pl.BlockSpec(memory_space=pltpu.SMEM),
            pl.BlockSpec((tq, LANES), lambda b, h, i: (b * nq + i, h)),
            pl.BlockSpec((seq, LANES), lambda b, h, i: (b, h)),
            pl.BlockSpec((seq, LANES), lambda b, h, i: (b, vcb + h)),
            pl.BlockSpec((tq, LANES), lambda b, h, i: (b * nq + i, gcb + h)),
            pl.BlockSpec((1, LANES), lambda b, h, i: (0, 0)),
        ],
        out_specs=pl.BlockSpec((tq, LANES), lambda b, h, i: (b * nq + i, h)),
        out_shape=jax.ShapeDtypeStruct((t, w), BF16),
        scratch_shapes=[pltpu.VMEM((2 * tq, LANES), BF16), pltpu.VMEM((seq, 2 * LANES), BF16),
                        sbuf, sbuf, sbuf, pbuf, pbuf, pbuf, stat, stat, stat, stat,
                        pltpu.VMEM((2 * tq, 2 * LANES), F32)],
        compiler_params=_params("parallel", "parallel", "arbitrary"),
        name="diff_attention",
    )(lam.reshape(1), qn, kn, proj, proj, subln_g.astype(F32).reshape(1, LANES))


def _proj_res_kernel(a_ref, b_ref, w_ref, x_ref, o_ref):
    kh = a_ref.shape[1]
    half = o_ref.shape[0] // 2
    for r0 in (0, half):
        rs = slice(r0, r0 + half)
        y = (jnp.dot(a_ref[rs, :], w_ref[0:kh, :], preferred_element_type=F32)
             + jnp.dot(b_ref[rs, :], w_ref[kh:2 * kh, :], preferred_element_type=F32))
        o_ref[rs, :] = x_ref[rs, :] + y


def _proj_residual(a, b, w, x, *, tm):
    m, n = x.shape
    kh = w.shape[0] // 2
    return pl.pallas_call(
        _proj_res_kernel,
        grid=(m // tm,),
        in_specs=[
            pl.BlockSpec((tm, kh), lambda i: (i, 0)),
            pl.BlockSpec((tm, kh), lambda i: (i, 0)),
            pl.BlockSpec(w.shape, lambda i: (0, 0), pipeline_mode=pl.Buffered(1)),
            pl.BlockSpec((tm, n), lambda i: (i, 0)),
        ],
        out_specs=pl.BlockSpec((tm, n), lambda i: (i, 0)),
        out_shape=jax.ShapeDtypeStruct((m, n), F32),
        compiler_params=_params("parallel"),
        name="proj_residual",
    )(a, b, w, x)


def _split_bf16(x):
    hi = x.astype(BF16)
    return hi, (x - hi.astype(F32)).astype(BF16)


def _s5_kernel(u_ref, d_ref, sw_ref, cc_ref, cw_ref, lr_ref, li_ref, y_ref, ut_ref, yt_ref, toep_ref,
               *, chunks_per_seq):
    p = S5_STATE
    gb, width, ncol = ut_ref.shape
    mm = cc_ref.shape[1]
    nl = width // mm
    nc = chunks_per_seq
    reps = ncol // LANES
    lane = lax.broadcasted_iota(jnp.int32, (p, ncol), 1)
    cpos = lane % nc

    def shifted(x, k):
        return jnp.where(cpos >= k, pltpu.roll(x, k, 1), 0.0)

    for b in range(ncol // nc):
        for l in range(nl):
            xt = u_ref[(b * nl + l) * nc:(b * nl + l + 1) * nc, :].astype(F32).T.astype(ut_ref.dtype)
            for g in range(gb):
                ut_ref[g, l * mm:(l + 1) * mm, b * nc:(b + 1) * nc] = xt[g * mm:(g + 1) * mm, :]

    for g in range(gb):
        u = ut_ref[g]
        sw_hi, sw_lo = _split_bf16(sw_ref[g])
        c_hi, c_lo = _split_bf16(cc_ref[g])
        kcat = (jnp.dot(c_hi, sw_hi, preferred_element_type=F32)
                + jnp.dot(c_hi, sw_lo, preferred_element_type=F32)
                + jnp.dot(c_lo, sw_hi, preferred_element_type=F32))
        z = jnp.concatenate([kcat, jnp.zeros_like(kcat)], axis=1)
        rolled = [z if r == 0 else pltpu.roll(z, 2 * width - r, 1) for r in range(0, LANES, mm)]
        for l in range(nl):
            off = (nl - 1 - l) * mm
            base = off - off % LANES
            strip = rolled[(off % LANES) // mm][:, base:base + width]
            toep_ref[l * mm:(l + 1) * mm, :] = strip.astype(toep_ref.dtype)
        s = jnp.dot(sw_hi, u, preferred_element_type=F32)
        xr, xi = s[:p], s[p:]
        lr = jnp.concatenate([lr_ref[g]] * reps, axis=1)
        li = jnp.concatenate([li_ref[g]] * reps, axis=1)
        k = 1
        while k < chunks_per_seq:
            sr, si = shifted(xr, k), shifted(xi, k)
            xr, xi = xr + (lr * sr - li * si), xi + (lr * si + li * sr)
            lr, li = lr * lr - li * li, 2.0 * (lr * li)
            k *= 2
        h = jnp.concatenate([shifted(xr, 1), shifted(xi, 1)], axis=0).astype(BF16)
        yt_ref[g] = (jnp.dot(toep_ref[...], u, preferred_element_type=F32)
                     + jnp.dot(cw_ref[g], h, preferred_element_type=F32))

    for b in range(ncol // nc):
        for l in range(nl):
            rows = slice((b * nl + l) * nc, (b * nl + l + 1) * nc)
            zt = jnp.concatenate([yt_ref[g, l * mm:(l + 1) * mm, b * nc:(b + 1) * nc] for g in range(gb)], axis=0)
            y = zt.T + d_ref[...] * u_ref[rows, :].astype(F32)
            y_ref[rows, :] = _gelu_tanh(y).astype(y_ref.dtype)


def _s5_tables(a_re, a_im, log_dt, b_re, b_im, c_re, c_im, chunk):
    g, p, m = b_re.shape
    dt = jnp.exp(log_dt.astype(F32))[:, None]
    a_re, a_im = a_re.astype(F32), a_im.astype(F32)
    mag = jnp.exp(a_re * dt)
    lb_re, lb_im = mag * jnp.cos(a_im * dt), mag * jnp.sin(a_im * dt)
    den = a_re * a_re + a_im * a_im
    nr, ni = lb_re - 1.0, lb_im
    fr = (nr * a_re + ni * a_im) / den
    fi = (ni * a_re - nr * a_im) / den
    bb_re = fr[..., None] * b_re - fi[..., None] * b_im
    bb_im = fr[..., None] * b_im + fi[..., None] * b_re
    d = jnp.arange(chunk + 1, dtype=F32)[None, :, None]
    pmag = jnp.exp((a_re * dt)[:, None, :] * d)
    pang = (a_im * dt)[:, None, :] * d
    pr, pi = pmag * jnp.cos(pang), pmag * jnp.sin(pang)
    c_re, c_im = c_re.astype(F32), c_im.astype(F32)
    hp = lax.Precision.HIGHEST
    rep_l = jnp.repeat(jnp.eye(chunk, dtype=F32)[::-1], m, axis=1)
    til_m = jnp.tile(jnp.eye(m, dtype=F32), (1, chunk))
    prt = jnp.einsum('gdp,dj->gpj', pr[:, :chunk], rep_l, precision=hp)
    pit = jnp.einsum('gdp,dj->gpj', pi[:, :chunk], rep_l, precision=hp)
    bbr = jnp.einsum('gpm,mj->gpj', bb_re, til_m, precision=hp)
    bbi = jnp.einsum('gpm,mj->gpj', bb_im, til_m, precision=hp)
    sw = jnp.concatenate([prt * bbr - pit * bbi, prt * bbi + pit * bbr], axis=1)
    cc = jnp.concatenate([c_re, -c_im], axis=-1)
    pr1, pi1 = pr[:, 1:, None, :], pi[:, 1:, None, :]
    cr, ci = c_re[:, None], c_im[:, None]
    cw = jnp.concatenate([cr * pr1 - ci * pi1, -(cr * pi1 + ci * pr1)], axis=-1)
    cw = cw.reshape(g, chunk * m, 2 * p)
    lam_r = jnp.broadcast_to(pr[:, chunk, :, None], (g, p, LANES))
    lam_i = jnp.broadcast_to(pi[:, chunk, :, None], (g, p, LANES))
    return sw, cc, cw.astype(BF16), lam_r, lam_i


def _s5(proj, d_skip, tables, *, batch, seq):
    sw, cc, cw, lam_r, lam_i = tables
    t = proj.shape[0]
    g, m = cc.shape[0], cc.shape[1]
    e = g * m
    chunk = sw.shape[2] // m
    nc = seq // chunk
    ncol = t // chunk
    gb = LANES // m
    up = proj.reshape(batch, nc, chunk, proj.shape[1])[..., :e].transpose(0, 2, 1, 3).reshape(t, e)
    wspec = lambda a: pl.BlockSpec((gb,) + a.shape[1:], lambda i: (i, 0, 0))
    yp = pl.pallas_call(
        functools.partial(_s5_kernel, chunks_per_seq=nc),
        grid=(g // gb,),
        in_specs=[pl.BlockSpec((t, LANES), lambda i: (0, i)), pl.BlockSpec((1, LANES), lambda i: (0, i)),
                  wspec(sw), wspec(cc), wspec(cw), wspec(lam_r), wspec(lam_i)],
        out_specs=pl.BlockSpec((t, LANES), lambda i: (0, i)),
        out_shape=jax.ShapeDtypeStruct((t, e), BF16),
        scratch_shapes=[pltpu.VMEM((gb, chunk * m, ncol), BF16), pltpu.VMEM((gb, chunk * m, ncol), F32),
                        pltpu.VMEM((chunk * m, chunk * m), BF16)],
        compiler_params=_params("parallel"),
        name="s5_chunked",
    )(up, d_skip.astype(F32).reshape(1, e), sw, cc, cw, lam_r, lam_i)
    return yp.reshape(batch, chunk, nc, e).transpose(0, 2, 1, 3).reshape(t, e)


def _glu_out_kernel(z_ref, gate_ref, wg_ref, b_ref, wo_ref, x_ref, o_ref):
    half = o_ref.shape[0] // 2
    for r0 in (0, half):
        rs = slice(r0, r0 + half)
        z = z_ref[rs, :]
        t = jnp.dot(z, wg_ref[...], preferred_element_type=F32) + b_ref[...]
        out = z.astype(F32) * jax.nn.sigmoid(t) * _silu(gate_ref[rs, :].astype(F32))
        o_ref[rs, :] = x_ref[rs, :] + jnp.dot(out.astype(BF16), wo_ref[...], preferred_element_type=F32)


def _glu_out(z, proj, w_glu, b_glu, w_out, x, *, tm):
    t, e = z.shape
    n = w_out.shape[1]
    resident = lambda a: pl.BlockSpec(a.shape, lambda i: (0, 0), pipeline_mode=pl.Buffered(1))
    return pl.pallas_call(
        _glu_out_kernel,
        grid=(t // tm,),
        in_specs=[
            pl.BlockSpec((tm, e), lambda i: (i, 0)),
            pl.BlockSpec((tm, e), lambda i: (i, 1)),
            resident(w_glu),
            pl.BlockSpec((1, e), lambda i: (0, 0)),
            resident(w_out),
            pl.BlockSpec((tm, n), lambda i: (i, 0)),
        ],
        out_specs=pl.BlockSpec((tm, n), lambda i: (i, 0)),
        out_shape=jax.ShapeDtypeStruct((t, n), F32),
        compiler_params=_params("parallel"),
        name="glu_out_residual",
    )(z, proj, w_glu, b_glu.astype(F32).reshape(1, e), w_out, x)


def _even_layer(x, layer_idx, norm_g, w_in, conv_w, conv_b, cln_g, cln_b, qn_g, kn_g,
                lam_q1, lam_k1, lam_q2, lam_k2, subln_g, w_out, *, batch, seq):
    conv_ch = conv_w.shape[1]
    proj = _rms_matmul(x, norm_g, w_in.astype(BF16), tm=512, nsplit=2)
    mix_a = _conv_mixer(proj, conv_w.astype(F32), conv_b.astype(F32), cln_g.astype(F32),
                        cln_b.astype(F32), seq=seq, tt=256)
    qn, kn = _qkv_prep(proj, qn_g, kn_g, seq=seq, tt=512, col0=3 * conv_ch)
    lam_init = 0.8 - 0.6 * math.exp(-0.3 * layer_idx)
    lam = (jnp.exp(jnp.sum(lam_q1.astype(F32) * lam_k1.astype(F32)))
           - jnp.exp(jnp.sum(lam_q2.astype(F32) * lam_k2.astype(F32))) + lam_init)
    qkv_w = qn.shape[1]
    mix_b = _diff_attention(lam, qn, kn, proj, subln_g, batch=batch, seq=seq, tq=512,
                            v_col0=3 * conv_ch + 2 * qkv_w, gate_col0=3 * conv_ch + 3 * qkv_w,
                            post_scale=1.0 - lam_init)
    return _proj_residual(mix_a, mix_b, w_out.astype(BF16), x, tm=512)


def _odd_layer(x, norm_g, w_in, a_re, a_im, log_dt, b_re, b_im, c_re, c_im, d_skip,
               w_glu, b_glu, w_out, *, batch, seq):
    proj = _rms_matmul(x, norm_g, w_in.astype(BF16), tm=512, nsplit=1)
    tables = _s5_tables(a_re, a_im, log_dt, b_re, b_im, c_re, c_im, S5_CHUNK)
    z = _s5(proj, d_skip, tables, batch=batch, seq=seq)
    return _glu_out(z, proj, w_glu.astype(BF16), b_glu, w_out.astype(BF16), x, tm=512)


def kernel(x, e_norm_g, e_w_in, e_conv_w, e_conv_b, e_cln_g, e_cln_b, e_qn_g, e_kn_g, e_lam_q1, e_lam_k1, e_lam_q2, e_lam_k2, e_subln_g, e_w_out, o_norm_g, o_w_in, o_A_re, o_A_im, o_log_dt, o_B_re, o_B_im, o_C_re, o_C_im, o_D, o_w_glu, o_b_glu, o_w_out):
    batch, seq, d_model = x.shape
    depth = e_norm_g.shape[0] + o_norm_g.shape[0]
    h = x.reshape(batch * seq, d_model)
    for layer in range(depth):
        j = layer // 2
        if layer % 2 == 0:
            h = _even_layer(h, layer, e_norm_g[j], e_w_in[j], e_conv_w[j], e_conv_b[j], e_cln_g[j],
                            e_cln_b[j], e_qn_g[j], e_kn_g[j], e_lam_q1[j], e_lam_k1[j], e_lam_q2[j],
                            e_lam_k2[j], e_subln_g[j], e_w_out[j], batch=batch, seq=seq)
        else:
            h = _odd_layer(h, o_norm_g[j], o_w_in[j], o_A_re[j], o_A_im[j], o_log_dt[j], o_B_re[j],
                           o_B_im[j], o_C_re[j], o_C_im[j], o_D[j], o_w_glu[j], o_b_glu[j], o_w_out[j],
                           batch=batch, seq=seq)
    return h.reshape(batch, seq, d_model)
```

```python
import functools
import math

import jax
import jax.numpy as jnp
from jax import lax
from jax.experimental import pallas as pl
from jax.experimental.pallas import tpu as pltpu

F32 = jnp.float32
BF16 = jnp.bfloat16

EPS = 1e-6
ROPE_THETA = 10000.0
CONV_W = 31
DA_HEADS = 8
DA_HEAD_DIM = 64
S5_GROUP = 16
S5_STATE = 64

LANES = 128
CONV_HALO = 32
CONV_ROWS = 32
ATT_ROWS = 64
S5_CHUNK = 32
VMEM_LIMIT = 56 * 1024 * 1024


def _params(*sem):
    return pltpu.CompilerParams(dimension_semantics=sem, vmem_limit_bytes=VMEM_LIMIT)


def _silu(x):
    return x * jax.nn.sigmoid(x)


def _gelu_tanh(x):
    return 0.5 * x * (1.0 + jnp.tanh(math.sqrt(2.0 / math.pi) * (x + 0.044715 * (x * x * x))))


def _rms_matmul_kernel(x_ref, g_ref, w_ref, o_ref):
    half = o_ref.shape[0] // 2
    for r0 in (0, half):
        rs = slice(r0, r0 + half)
        x = x_ref[rs, :]
        ms = jnp.mean(x * x, axis=-1, keepdims=True)
        h = (x * lax.rsqrt(ms + EPS) * g_ref[...]).astype(BF16)
        o_ref[rs, :] = jnp.dot(h, w_ref[...], preferred_element_type=F32).astype(o_ref.dtype)


def _rms_matmul(x, g, w, *, tm, nsplit):
    m, d = x.shape
    n = w.shape[1]
    tn = n // nsplit
    return pl.pallas_call(
        _rms_matmul_kernel,
        grid=(nsplit, m // tm),
        in_specs=[
            pl.BlockSpec((tm, d), lambda j, i: (i, 0)),
            pl.BlockSpec((1, d), lambda j, i: (0, 0)),
            pl.BlockSpec((d, tn), lambda j, i: (0, j), pipeline_mode=pl.Buffered(1)),
        ],
        out_specs=pl.BlockSpec((tm, tn), lambda j, i: (i, j)),
        out_shape=jax.ShapeDtypeStruct((m, n), BF16),
        compiler_params=_params("parallel", "parallel"),
        name="rms_matmul",
    )(x, g.reshape(1, d), w)


def _conv_kernel(val_ref, glu_ref, gate_ref, hval_ref, hglu_ref, w_ref, cb_ref, lg_ref, lb_ref,
                 o_ref, uext_ref, *, tt, tiles_per_seq):
    c = o_ref.shape[-1]
    nlt = c // LANES
    first = (pl.program_id(0) % tiles_per_seq) == 0
    f32 = lambda ref, idx=Ellipsis: ref[idx].astype(F32)
    halo = jnp.where(first, 0.0, f32(hval_ref) * jax.nn.sigmoid(f32(hglu_ref)))
    cur = f32(val_ref) * jax.nn.sigmoid(f32(glu_ref))
    for j in range(nlt):
        sl = slice(j * LANES, (j + 1) * LANES)
        uext_ref[j, pl.ds(0, CONV_HALO, stride=2), :] = halo[:, sl]
        uext_ref[j, pl.ds(2 * CONV_HALO, tt, stride=2), :] = cur[:, sl]
    lg = lg_ref[...]
    lb = lb_ref[...]

    def chunk(r, carry):
        base = pl.multiple_of(r * CONV_ROWS, CONV_ROWS)
        accs = []
        for j in range(nlt):
            sl = slice(j * LANES, (j + 1) * LANES)
            acc = jnp.broadcast_to(cb_ref[:, sl], (CONV_ROWS, LANES))
            for k in range(CONV_W):
                row = base + (CONV_HALO - CONV_W + 1 + k)
                acc = acc + w_ref[k:k + 1, sl] * uext_ref[j, pl.ds(2 * row, CONV_ROWS, stride=2), :]
            accs.append(acc)
        acc = jnp.concatenate(accs, axis=1)
        mean = jnp.mean(acc, axis=-1, keepdims=True)
        xc = acc - mean
        var = jnp.mean(xc * xc, axis=-1, keepdims=True)
        y = _silu(xc * lax.rsqrt(var + EPS) * lg + lb)
        gate = gate_ref[pl.ds(base, CONV_ROWS), :].astype(F32)
        o_ref[pl.ds(base, CONV_ROWS), :] = (y * _silu(gate)).astype(o_ref.dtype)
        return carry

    lax.fori_loop(0, tt // CONV_ROWS, chunk, 0, unroll=2)


def _conv_mixer(proj, conv_w, conv_b, ln_g, ln_b, *, seq, tt):
    t = proj.shape[0]
    c = conv_w.shape[1]
    hb = tt // CONV_HALO
    row = lambda a: a.reshape(1, c)
    return pl.pallas_call(
        functools.partial(_conv_kernel, tt=tt, tiles_per_seq=seq // tt),
        grid=(t // tt,),
        in_specs=[
            pl.BlockSpec((tt, c), lambda i: (i, 0)),
            pl.BlockSpec((tt, c), lambda i: (i, 1)),
            pl.BlockSpec((tt, c), lambda i: (i, 2)),
            pl.BlockSpec((CONV_HALO, c), lambda i: (jnp.maximum(i * hb - 1, 0), 0)),
            pl.BlockSpec((CONV_HALO, c), lambda i: (jnp.maximum(i * hb - 1, 0), 1)),
            pl.BlockSpec((CONV_W, c), lambda i: (0, 0)),
            pl.BlockSpec((1, c), lambda i: (0, 0)),
            pl.BlockSpec((1, c), lambda i: (0, 0)),
            pl.BlockSpec((1, c), lambda i: (0, 0)),
        ],
        out_specs=pl.BlockSpec((tt, c), lambda i: (i, 0)),
        out_shape=jax.ShapeDtypeStruct((t, c), BF16),
        scratch_shapes=[pltpu.VMEM((c // LANES, 2 * (CONV_HALO + tt), LANES), F32)],
        compiler_params=_params("parallel"),
        name="conv_mixer",
    )(proj, proj, proj, proj, proj, conv_w, row(conv_b), row(ln_g), row(ln_b))


def _qkv_kernel(q_ref, k_ref, cos_ref, sin_ref, qg_ref, kg_ref, ones_ref, qo_ref, ko_ref, *, q_scale):
    cos = cos_ref[...]
    sin = sin_ref[...]
    ones = ones_ref[...]
    lane = lax.broadcasted_iota(jnp.int32, cos.shape, 1)
    first_half = (lane % DA_HEAD_DIM) < (DA_HEAD_DIM // 2)

    def prep(x, g, scale):
        x2 = x * x
        hi = x2.astype(BF16)
        lo = (x2 - hi.astype(F32)).astype(BF16)
        ss = (jnp.dot(hi, ones, preferred_element_type=F32)
              + jnp.dot(lo, ones, preferred_element_type=F32))
        y = x * lax.rsqrt(ss * (1.0 / DA_HEAD_DIM) + EPS) * g
        partner = jnp.where(first_half, pltpu.roll(y, LANES - DA_HEAD_DIM // 2, 1),
                            pltpu.roll(y, DA_HEAD_DIM // 2, 1))
        return (y * cos + partner * sin) * scale

    for h in range(q_ref.shape[-1] // LANES):
        sl = slice(h * LANES, (h + 1) * LANES)
        qo_ref[:, sl] = prep(q_ref[:, sl].astype(F32), qg_ref[...], q_scale).astype(qo_ref.dtype)
        ko_ref[:, sl] = prep(k_ref[:, sl].astype(F32), kg_ref[...], 1.0).astype(ko_ref.dtype)


def _qkv_prep(proj, qn_g, kn_g, *, seq, tt, col0):
    t = proj.shape[0]
    w = DA_HEADS * 2 * DA_HEAD_DIM
    half = DA_HEAD_DIM // 2
    freqs = ROPE_THETA ** (-jnp.arange(half, dtype=F32) / half)
    ang = jnp.arange(seq, dtype=F32)[:, None] * freqs[None, :]
    cos = jnp.tile(jnp.cos(ang), (1, LANES // half))
    sin = jnp.tile(jnp.concatenate([-jnp.sin(ang), jnp.sin(ang)], axis=-1), (1, LANES // DA_HEAD_DIM))
    blk = jnp.arange(LANES) // DA_HEAD_DIM
    ones = (blk[:, None] == blk[None, :]).astype(BF16)
    tile_g = lambda g: jnp.tile(g.astype(F32), LANES // DA_HEAD_DIM).reshape(1, LANES)
    cb = col0 // w
    nseq = seq // tt
    out = jax.ShapeDtypeStruct((t, w), BF16)
    return pl.pallas_call(
        functools.partial(_qkv_kernel, q_scale=math.log2(math.e) * DA_HEAD_DIM ** -0.5),
        grid=(t // tt,),
        in_specs=[
            pl.BlockSpec((tt, w), lambda i: (i, cb)),
            pl.BlockSpec((tt, w), lambda i: (i, cb + 1)),
            pl.BlockSpec((tt, LANES), lambda i: (i % nseq, 0)),
            pl.BlockSpec((tt, LANES), lambda i: (i % nseq, 0)),
            pl.BlockSpec((1, LANES), lambda i: (0, 0)),
            pl.BlockSpec((1, LANES), lambda i: (0, 0)),
            pl.BlockSpec((LANES, LANES), lambda i: (0, 0)),
        ],
        out_specs=[pl.BlockSpec((tt, w), lambda i: (i, 0))] * 2,
        out_shape=[out, out],
        compiler_params=_params("parallel"),
        name="qkv_prep",
    )(proj, proj, cos, sin, tile_g(qn_g), tile_g(kn_g), ones)


def _attn_kernel(lam_ref, q_ref, k_ref, v_ref, bg_ref, sg_ref, o_ref,
                 qs_ref, vaug_ref, s0_ref, s1_ref, s2_ref, p0_ref, p1_ref, p2_ref, m_ref,
                 a0_ref, a1_ref, a2_ref, acc_ref, *, tq, post_scale):
    i = pl.program_id(2)
    s_refs, p_refs, a_refs = (s0_ref, s1_ref, s2_ref), (p0_ref, p1_ref, p2_ref), (a0_ref, a1_ref, a2_ref)

    @pl.when(i == 0)
    def _():
        vaug_ref[:, 0:LANES] = v_ref[...]
        vaug_ref[:, LANES:2 * LANES] = jnp.ones(v_ref.shape, vaug_ref.dtype)

    q = q_ref[...]
    lane = lax.broadcasted_iota(jnp.int32, q.shape, 1)
    zero = jnp.zeros_like(q)
    qs_ref[0:tq, :] = jnp.where(lane < DA_HEAD_DIM, q, zero)
    qs_ref[tq:2 * tq, :] = jnp.where(lane >= DA_HEAD_DIM, q, zero)
    m_ref[...] = jnp.full(m_ref.shape, -jnp.inf, F32)
    acc_ref[...] = jnp.zeros(acc_ref.shape, F32)
    reps = tq // LANES

    def scores(j, slot):
        start = pl.multiple_of(j * tq, tq)
        s_refs[slot][...] = lax.dot_general(qs_ref[...], k_ref[pl.ds(start, tq), :],
                                            (((1,), (1,)), ((), ())), preferred_element_type=F32)

    def softmax(slot, masked):
        s_ref, p_ref, a_ref = s_refs[slot], p_refs[slot], a_refs[slot]
        for r0 in range(0, 2 * tq, ATT_ROWS):
            rs = slice(r0, r0 + ATT_ROWS)
            s = s_ref[rs, :]
            if masked:
                row = (r0 % tq) + lax.broadcasted_iota(jnp.int32, s.shape, 0)
                col = lax.broadcasted_iota(jnp.int32, s.shape, 1)
                s = jnp.where(col <= row, s, -jnp.inf)
            m_prev = m_ref[rs, :]
            m_new = jnp.maximum(m_prev, jnp.max(s, axis=-1, keepdims=True))
            a_ref[rs, :] = jnp.exp2(m_prev - m_new)
            m_ref[rs, :] = m_new
            p_ref[rs, :] = jnp.exp2(s - jnp.concatenate([m_new] * reps, axis=1)).astype(p_ref.dtype)

    def accumulate(j, slot):
        start = pl.multiple_of(j * tq, tq)
        alpha = a_refs[slot][...]
        acc_ref[...] = jnp.concatenate([alpha, alpha], axis=1) * acc_ref[...] + jnp.dot(
            p_refs[slot][...], vaug_ref[pl.ds(start, tq), :], preferred_element_type=F32)

    scores(0, 0)

    @pl.when(i == 0)
    def _():
        softmax(0, True)
        accumulate(0, 0)

    @pl.when(i > 0)
    def _():
        scores(1, 1)
        softmax(0, False)
        npairs = (i - 1) // 2

        def pair(tt, carry):
            t = 2 * tt
            scores(t + 2, 0)
            softmax(1, False)
            accumulate(t, 0)
            scores(t + 3, 1)
            softmax(0, False)
            accumulate(t + 1, 1)
            return carry

        lax.fori_loop(0, npairs, pair, 0)

        @pl.when(i % 2 == 1)
        def _():
            softmax(1, True)
            accumulate(i - 1, 0)
            accumulate(i, 1)

        @pl.when(i % 2 == 0)
        def _():
            scores(i, 2)
            softmax(1, False)
            accumulate(i - 2, 0)
            softmax(2, True)
            accumulate(i - 1, 1)
            accumulate(i, 2)

    o = acc_ref[:, 0:LANES] / acc_ref[:, LANES:2 * LANES]
    d = o[:tq] - lam_ref[0] * o[tq:]
    ms = jnp.mean(d * d, axis=-1, keepdims=True)
    y = d * lax.rsqrt(ms + EPS) * sg_ref[...] * post_scale
    o_ref[...] = (y * _silu(bg_ref[...].astype(F32))).astype(o_ref.dtype)


def _diff_attention(lam, qn, kn, proj, subln_g, *, batch, seq, tq, v_col0, gate_col0, post_scale):
    t, w = qn.shape
    nq = seq // tq
    vcb = v_col0 // LANES
    gcb = gate_col0 // LANES
    stat = pltpu.VMEM((2 * tq, LANES), F32)
    sbuf = pltpu.VMEM((2 * tq, tq), F32)
    pbuf = pltpu.VMEM((2 * tq, tq), BF16)
    return pl.pallas_call(
        functools.partial(_attn_kernel, tq=tq, post_scale=post_scale),
        grid=(batch, DA_HEADS, nq),
        in_specs=[
            pl.BlockSpec(memory_space=pltpu.SMEM),
            pl.BlockSpec((tq, LANES), lambda b, h, i: (b * nq + i, h)),
            pl.BlockSpec((seq, LANES), lambda b, h, i: (b, h)),
            pl.BlockSpec((seq, LANES), lambda b, h, i: (b, vcb + h)),
            pl.BlockSpec((tq, LANES), lambda b, h, i: (b * nq + i, gcb + h)),
            pl.BlockSpec((1, LANES), lambda b, h, i: (0, 0)),
        ],
        out_specs=pl.BlockSpec((tq, LANES), lambda b, h, i: (b * nq + i, h)),
        out_shape=jax.ShapeDtypeStruct((t, w), BF16),
        scratch_shapes=[pltpu.VMEM((2 * tq, LANES), BF16), pltpu.VMEM((seq, 2 * LANES), BF16),
                        sbuf, sbuf, sbuf, pbuf, pbuf, pbuf, stat, stat, stat, stat,
                        pltpu.VMEM((2 * tq, 2 * LANES), F32)],
        compiler_params=_params("parallel", "parallel", "arbitrary"),
        name="diff_attention",
    )(lam.reshape(1), qn, kn, proj, proj, subln_g.astype(F32).reshape(1, LANES))


def _proj_res_kernel(a_ref, b_ref, w_ref, x_ref, o_ref):
    kh = a_ref.shape[1]
    half = o_ref.shape[0] // 2
    for r0 in (0, half):
        rs = slice(r0, r0 + half)
        y = (jnp.dot(a_ref[rs, :], w_ref[0:kh, :], preferred_element_type=F32)
             + jnp.dot(b_ref[rs, :], w_ref[kh:2 * kh, :], preferred_element_type=F32))
        o_ref[rs, :] = x_ref[rs, :] + y


def _proj_residual(a, b, w, x, *, tm):
    m, n = x.shape
    kh = w.shape[0] // 2
    return pl.pallas_call(
        _proj_res_kernel,
        grid=(m // tm,),
        in_specs=[
            pl.BlockSpec((tm, kh), lambda i: (i, 0)),
            pl.BlockSpec((tm, kh), lambda i: (i, 0)),
            pl.BlockSpec(w.shape, lambda i: (0, 0), pipeline_mode=pl.Buffered(1)),
            pl.BlockSpec((tm, n), lambda i: (i, 0)),
        ],
        out_specs=pl.BlockSpec((tm, n), lambda i: (i, 0)),
        out_shape=jax.ShapeDtypeStruct((m, n), F32),
        compiler_params=_params("parallel"),
        name="proj_residual",
    )(a, b, w, x)


def _split_bf16(x):
    hi = x.astype(BF16)
    return hi, (x - hi.astype(F32)).astype(BF16)


def _expand_lanes(x, onehot):
    hi, lo = _split_bf16(x)
    return (jnp.dot(hi, onehot, preferred_element_type=F32) + jnp.dot(lo, onehot, preferred_element_type=F32))


def _s5_kernel(u_ref, d_ref, pvr_ref, pvi_ref, bbr_ref, bbi_ref, cc_ref, cc2_ref, prx_ref, pix_ref,
               lr_ref, li_ref, ohl_ref, ohm_ref, y_ref, ut_ref, yt_ref, toep_ref, cw_ref,
               *, chunks_per_seq):
    p = S5_STATE
    gb, width, ncol = ut_ref.shape
    mm = cc_ref.shape[1]
    nl = width // mm
    nc = chunks_per_seq
    reps = ncol // LANES
    lane = lax.broadcasted_iota(jnp.int32, (p, ncol), 1)
    cpos = lane % nc

    def shifted(x, k):
        return jnp.where(cpos >= k, pltpu.roll(x, k, 1), 0.0)

    for b in range(ncol // nc):
        for l in range(nl):
            xt = u_ref[(b * nl + l) * nc:(b * nl + l + 1) * nc, :].astype(F32).T.astype(ut_ref.dtype)
            for g in range(gb):
                ut_ref[g, l * mm:(l + 1) * mm, b * nc:(b + 1) * nc] = xt[g * mm:(g + 1) * mm, :]

    for g in range(gb):
        u = ut_ref[g]
        prt, pit = _expand_lanes(pvr_ref[g], ohl_ref[...]), _expand_lanes(pvi_ref[g], ohl_ref[...])
        bbr, bbi = _expand_lanes(bbr_ref[g], ohm_ref[...]), _expand_lanes(bbi_ref[g], ohm_ref[...])
        sw_hi, sw_lo = _split_bf16(jnp.concatenate([prt * bbr - pit * bbi, prt * bbi + pit * bbr], axis=0))
        c_hi, c_lo = _split_bf16(cc_ref[g])
        for l in range(nl):
            cw_ref[l * mm:(l + 1) * mm, :] = (cc_ref[g] * prx_ref[g, l:l + 1, :]
                                              + cc2_ref[g] * pix_ref[g, l:l + 1, :]).astype(cw_ref.dtype)
        kcat = (jnp.dot(c_hi, sw_hi, preferred_element_type=F32)
                + jnp.dot(c_hi, sw_lo, preferred_element_type=F32)
                + jnp.dot(c_lo, sw_hi, preferred_element_type=F32))
        z = jnp.concatenate([kcat, jnp.zeros_like(kcat)], axis=1)
        rolled = [z if r == 0 else pltpu.roll(z, 2 * width - r, 1) for r in range(0, LANES, mm)]
        for l in range(nl):
            off = (nl - 1 - l) * mm
            base = off - off % LANES
            strip = rolled[(off % LANES) // mm][:, base:base + width]
            toep_ref[l * mm:(l + 1) * mm, :] = strip.astype(toep_ref.dtype)
        s = jnp.dot(sw_hi, u, preferred_element_type=F32)
        xr, xi = s[:p], s[p:]
        lr = jnp.concatenate([lr_ref[g]] * reps, axis=1)
        li = jnp.concatenate([li_ref[g]] * reps, axis=1)
        k = 1
        while k < chunks_per_seq:
            sr, si = shifted(xr, k), shifted(xi, k)
            xr, xi = xr + (lr * sr - li * si), xi + (lr * si + li * sr)
            lr, li = lr * lr - li * li, 2.0 * (lr * li)
            k *= 2
        h = jnp.concatenate([shifted(xr, 1), shifted(xi, 1)], axis=0).astype(BF16)
        yt_ref[g] = (jnp.dot(toep_ref[...], u, preferred_element_type=F32)
                     + jnp.dot(cw_ref[...], h, preferred_element_type=F32))

    for b in range(ncol // nc):
        for l in range(nl):
            rows = slice((b * nl + l) * nc, (b * nl + l + 1) * nc)
            zt = jnp.concatenate([yt_ref[g, l * mm:(l + 1) * mm, b * nc:(b + 1) * nc] for g in range(gb)], axis=0)
            y = zt.T + d_ref[...] * u_ref[rows, :].astype(F32)
            y_ref[rows, :] = _gelu_tanh(y).astype(y_ref.dtype)


def _s5_tables(a_re, a_im, log_dt, b_re, b_im, c_re, c_im, chunk):
    g, p, m = b_re.shape
    dt = jnp.exp(log_dt.astype(F32))[:, None]
    a_re, a_im = a_re.astype(F32), a_im.astype(F32)
    mag = jnp.exp(a_re * dt)
    lb_re, lb_im = mag * jnp.cos(a_im * dt), mag * jnp.sin(a_im * dt)
    den = a_re * a_re + a_im * a_im
    nr, ni = lb_re - 1.0, lb_im
    fr = (nr * a_re + ni * a_im) / den
    fi = (ni * a_re - nr * a_im) / den
    bb_re = fr[..., None] * b_re - fi[..., None] * b_im
    bb_im = fr[..., None] * b_im + fi[..., None] * b_re
    d = jnp.arange(chunk + 1, dtype=F32)[None, :, None]
    pmag = jnp.exp((a_re * dt)[:, None, :] * d)
    pang = (a_im * dt)[:, None, :] * d
    pr, pi = pmag * jnp.cos(pang), pmag * jnp.sin(pang)
    c_re, c_im = c_re.astype(F32), c_im.astype(F32)
    pad = lambda a: jnp.pad(a, ((0, 0), (0, 0), (0, LANES - a.shape[2])))
    rev = lambda a: pad(jnp.flip(a[:, :chunk], axis=1).transpose(0, 2, 1))
    onehot_l = jnp.repeat(jnp.eye(LANES, chunk, dtype=BF16), m, axis=1)
    onehot_m = jnp.tile(jnp.eye(LANES, m, dtype=BF16), (1, chunk))
    cc = jnp.concatenate([c_re, -c_im], axis=-1)
    cc2 = jnp.concatenate([-c_im, -c_re], axis=-1)
    prx = jnp.concatenate([pr[:, 1:], pr[:, 1:]], axis=-1)
    pix = jnp.concatenate([pi[:, 1:], pi[:, 1:]], axis=-1)
    lam_r = jnp.broadcast_to(pr[:, chunk, :, None], (g, p, LANES))
    lam_i = jnp.broadcast_to(pi[:, chunk, :, None], (g, p, LANES))
    return (rev(pr), rev(pi), pad(bb_re), pad(bb_im), cc, cc2, prx, pix, lam_r, lam_i), (onehot_l, onehot_m)


def _s5(proj, d_skip, tables, *, batch, seq):
    per_group, onehots = tables
    cc, prx = per_group[4], per_group[6]
    t = proj.shape[0]
    g, m = cc.shape[0], cc.shape[1]
    e = g * m
    chunk = prx.shape[1]
    nc = seq // chunk
    ncol = t // chunk
    gb = LANES // m
    up = proj.reshape(batch, nc, chunk, proj.shape[1])[..., :e].transpose(0, 2, 1, 3).reshape(t, e)
    wspec = lambda a: pl.BlockSpec((gb,) + a.shape[1:], lambda i: (i, 0, 0))
    const = lambda a: pl.BlockSpec(a.shape, lambda i: (0, 0))
    yp = pl.pallas_call(
        functools.partial(_s5_kernel, chunks_per_seq=nc),
        grid=(g // gb,),
        in_specs=([pl.BlockSpec((t, LANES), lambda i: (0, i)), pl.BlockSpec((1, LANES), lambda i: (0, i))]
                  + [wspec(a) for a in per_group] + [const(a) for a in onehots]),
        out_specs=pl.BlockSpec((t, LANES), lambda i: (0, i)),
        out_shape=jax.ShapeDtypeStruct((t, e), BF16),
        scratch_shapes=[pltpu.VMEM((gb, chunk * m, ncol), BF16), pltpu.VMEM((gb, chunk * m, ncol), F32),
                        pltpu.VMEM((chunk * m, chunk * m), BF16), pltpu.VMEM((chunk * m, 2 * S5_STATE), BF16)],
        compiler_params=_params("parallel"),
        name="s5_chunked",
    )(up, d_skip.astype(F32).reshape(1, e), *per_group, *onehots)
    return yp.reshape(batch, chunk, nc, e).transpose(0, 2, 1, 3).reshape(t, e)


def _glu_out_kernel(z_ref, gate_ref, wg_ref, b_ref, wo_ref, x_ref, o_ref):
    half = o_ref.shape[0] // 2
    for r0 in (0, half):
        rs = slice(r0, r0 + half)
        z = z_ref[rs, :]
        t = jnp.dot(z, wg_ref[...], preferred_element_type=F32) + b_ref[...]
        out = z.astype(F32) * jax.nn.sigmoid(t) * _silu(gate_ref[rs, :].astype(F32))
        o_ref[rs, :] = x_ref[rs, :] + jnp.dot(out.astype(BF16), wo_ref[...], preferred_element_type=F32)


def _glu_out(z, proj, w_glu, b_glu, w_out, x, *, tm):
    t, e = z.shape
    n = w_out.shape[1]
    resident = lambda a: pl.BlockSpec(a.shape, lambda i: (0, 0), pipeline_mode=pl.Buffered(1))
    return pl.pallas_call(
        _glu_out_kernel,
        grid=(t // tm,),
        in_specs=[
            pl.BlockSpec((tm, e), lambda i: (i, 0)),
            pl.BlockSpec((tm, e), lambda i: (i, 1)),
            resident(w_glu),
            pl.BlockSpec((1, e), lambda i: (0, 0)),
            resident(w_out),
            pl.BlockSpec((tm, n), lambda i: (i, 0)),
        ],
        out_specs=pl.BlockSpec((tm, n), lambda i: (i, 0)),
        out_shape=jax.ShapeDtypeStruct((t, n), F32),
        compiler_params=_params("parallel"),
        name="glu_out_residual",
    )(z, proj, w_glu, b_glu.astype(F32).reshape(1, e), w_out, x)


def _even_layer(x, layer_idx, norm_g, w_in, conv_w, conv_b, cln_g, cln_b, qn_g, kn_g,
                lam_q1, lam_k1, lam_q2, lam_k2, subln_g, w_out, *, batch, seq):
    conv_ch = conv_w.shape[1]
    proj = _rms_matmul(x, norm_g, w_in.astype(BF16), tm=512, nsplit=2)
    mix_a = _conv_mixer(proj, conv_w.astype(F32), conv_b.astype(F32), cln_g.astype(F32),
                        cln_b.astype(F32), seq=seq, tt=256)
    qn, kn = _qkv_prep(proj, qn_g, kn_g, seq=seq, tt=512, col0=3 * conv_ch)
    lam_init = 0.8 - 0.6 * math.exp(-0.3 * layer_idx)
    lam = (jnp.exp(jnp.sum(lam_q1.astype(F32) * lam_k1.astype(F32)))
           - jnp.exp(jnp.sum(lam_q2.astype(F32) * lam_k2.astype(F32))) + lam_init)
    qkv_w = qn.shape[1]
    mix_b = _diff_attention(lam, qn, kn, proj, subln_g, batch=batch, seq=seq, tq=512,
                            v_col0=3 * conv_ch + 2 * qkv_w, gate_col0=3 * conv_ch + 3 * qkv_w,
                            post_scale=1.0 - lam_init)
    return _proj_residual(mix_a, mix_b, w_out.astype(BF16), x, tm=512)


def _odd_layer(x, norm_g, w_in, a_re, a_im, log_dt, b_re, b_im, c_re, c_im, d_skip,
               w_glu, b_glu, w_out, *, batch, seq):
    proj = _rms_matmul(x, norm_g, w_in.astype(BF16), tm=512, nsplit=1)
    tables = _s5_tables(a_re, a_im, log_dt, b_re, b_im, c_re, c_im, S5_CHUNK)
    z = _s5(proj, d_skip, tables, batch=batch, seq=seq)
    return _glu_out(z, proj, w_glu.astype(BF16), b_glu, w_out.astype(BF16), x, tm=512)


def kernel(x, e_norm_g, e_w_in, e_conv_w, e_conv_b, e_cln_g, e_cln_b, e_qn_g, e_kn_g, e_lam_q1, e_lam_k1, e_lam_q2, e_lam_k2, e_subln_g, e_w_out, o_norm_g, o_w_in, o_A_re, o_A_im, o_log_dt, o_B_re, o_B_im, o_C_re, o_C_im, o_D, o_w_glu, o_b_glu, o_w_out):
    batch, seq, d_model = x.shape
    depth = e_norm_g.shape[0] + o_norm_g.shape[0]
    h = x.reshape(batch * seq, d_model)
    for layer in range(depth):
        j = layer // 2
        if layer % 2 == 0:
            h = _even_layer(h, layer, e_norm_g[j], e_w_in[j], e_conv_w[j], e_conv_b[j], e_cln_g[j],
                            e_cln_b[j], e_qn_g[j], e_kn_g[j], e_lam_q1[j], e_lam_k1[j], e_lam_q2[j],
                            e_lam_k2[j], e_subln_g[j], e_w_out[j], batch=batch, seq=seq)
        else:
            h = _odd_layer(h, o_norm_g[j], o_w_in[j], o_A_re[j], o_A_im[j], o_log_dt[j], o_B_re[j],
                           o_B_im[j], o_C_re[j], o_C_im[j], o_D[j], o_w_glu[j], o_b_glu[j], o_w_out[j],
                           batch=batch, seq=seq)
    return h.reshape(batch, seq, d_model)
```

```python
import functools
import math

import jax
import jax.numpy as jnp
from jax import lax
from jax.experimental import pallas as pl
from jax.experimental.pallas import tpu as pltpu

F32 = jnp.float32
BF16 = jnp.bfloat16

EPS = 1e-6
ROPE_THETA = 10000.0
CONV_W = 31
DA_HEADS = 8
DA_HEAD_DIM = 64
S5_GROUP = 16
S5_STATE = 64

LANES = 128
CONV_HALO = 32
CONV_ROWS = 32
ATT_ROWS = 64
ATT_FIXED_SHIFT_MAX = 60.0
ATT_Q_SCALE = math.log2(math.e) * DA_HEAD_DIM ** -0.5
S5_CHUNK = 32
VMEM_LIMIT = 56 * 1024 * 1024


def _params(*sem):
    return pltpu.CompilerParams(dimension_semantics=sem, vmem_limit_bytes=VMEM_LIMIT)


def _silu(x):
    return x * jax.nn.sigmoid(x)


def _gelu_tanh(x):
    return 0.5 * x * (1.0 + jnp.tanh(math.sqrt(2.0 / math.pi) * (x + 0.044715 * (x * x * x))))


def _rms_matmul_kernel(x_ref, g_ref, w_ref, o_ref):
    half = o_ref.shape[0] // 2
    for r0 in (0, half):
        rs = slice(r0, r0 + half)
        x = x_ref[rs, :]
        ms = jnp.mean(x * x, axis=-1, keepdims=True)
        h = (x * lax.rsqrt(ms + EPS) * g_ref[...]).astype(BF16)
        o_ref[rs, :] = jnp.dot(h, w_ref[...], preferred_element_type=F32).astype(o_ref.dtype)


def _rms_matmul(x, g, w, *, tm, nsplit):
    m, d = x.shape
    n = w.shape[1]
    tn = n // nsplit
    return pl.pallas_call(
        _rms_matmul_kernel,
        grid=(nsplit, m // tm),
        in_specs=[
            pl.BlockSpec((tm, d), lambda j, i: (i, 0)),
            pl.BlockSpec((1, d), lambda j, i: (0, 0)),
            pl.BlockSpec((d, tn), lambda j, i: (0, j), pipeline_mode=pl.Buffered(1)),
        ],
        out_specs=pl.BlockSpec((tm, tn), lambda j, i: (i, j)),
        out_shape=jax.ShapeDtypeStruct((m, n), BF16),
        compiler_params=_params("parallel", "parallel"),
        name="rms_matmul",
    )(x, g.reshape(1, d), w)


def _conv_kernel(val_ref, glu_ref, gate_ref, hval_ref, hglu_ref, w_ref, cb_ref, lg_ref, lb_ref,
                 o_ref, uext_ref, *, tt, tiles_per_seq):
    c = o_ref.shape[-1]
    nlt = c // LANES
    first = (pl.program_id(0) % tiles_per_seq) == 0
    f32 = lambda ref, idx=Ellipsis: ref[idx].astype(F32)
    halo = jnp.where(first, 0.0, f32(hval_ref) * jax.nn.sigmoid(f32(hglu_ref)))
    cur = f32(val_ref) * jax.nn.sigmoid(f32(glu_ref))
    for j in range(nlt):
        sl = slice(j * LANES, (j + 1) * LANES)
        uext_ref[j, pl.ds(0, CONV_HALO, stride=2), :] = halo[:, sl]
        uext_ref[j, pl.ds(2 * CONV_HALO, tt, stride=2), :] = cur[:, sl]
    lg = lg_ref[...]
    lb = lb_ref[...]

    def chunk(r, carry):
        base = pl.multiple_of(r * CONV_ROWS, CONV_ROWS)
        accs = []
        for j in range(nlt):
            sl = slice(j * LANES, (j + 1) * LANES)
            acc = jnp.broadcast_to(cb_ref[:, sl], (CONV_ROWS, LANES))
            for k in range(CONV_W):
                row = base + (CONV_HALO - CONV_W + 1 + k)
                acc = acc + w_ref[k:k + 1, sl] * uext_ref[j, pl.ds(2 * row, CONV_ROWS, stride=2), :]
            accs.append(acc)
        acc = jnp.concatenate(accs, axis=1)
        mean = jnp.mean(acc, axis=-1, keepdims=True)
        xc = acc - mean
        var = jnp.mean(xc * xc, axis=-1, keepdims=True)
        y = _silu(xc * lax.rsqrt(var + EPS) * lg + lb)
        gate = gate_ref[pl.ds(base, CONV_ROWS), :].astype(F32)
        o_ref[pl.ds(base, CONV_ROWS), :] = (y * _silu(gate)).astype(o_ref.dtype)
        return carry

    lax.fori_loop(0, tt // CONV_ROWS, chunk, 0, unroll=2)


def _conv_mixer(proj, conv_w, conv_b, ln_g, ln_b, *, seq, tt):
    t = proj.shape[0]
    c = conv_w.shape[1]
    hb = tt // CONV_HALO
    row = lambda a: a.reshape(1, c)
    return pl.pallas_call(
        functools.partial(_conv_kernel, tt=tt, tiles_per_seq=seq // tt),
        grid=(t // tt,),
        in_specs=[
            pl.BlockSpec((tt, c), lambda i: (i, 0)),
            pl.BlockSpec((tt, c), lambda i: (i, 1)),
            pl.BlockSpec((tt, c), lambda i: (i, 2)),
            pl.BlockSpec((CONV_HALO, c), lambda i: (jnp.maximum(i * hb - 1, 0), 0)),
            pl.BlockSpec((CONV_HALO, c), lambda i: (jnp.maximum(i * hb - 1, 0), 1)),
            pl.BlockSpec((CONV_W, c), lambda i: (0, 0)),
            pl.BlockSpec((1, c), lambda i: (0, 0)),
            pl.BlockSpec((1, c), lambda i: (0, 0)),
            pl.BlockSpec((1, c), lambda i: (0, 0)),
        ],
        out_specs=pl.BlockSpec((tt, c), lambda i: (i, 0)),
        out_shape=jax.ShapeDtypeStruct((t, c), BF16),
        scratch_shapes=[pltpu.VMEM((c // LANES, 2 * (CONV_HALO + tt), LANES), F32)],
        compiler_params=_params("parallel"),
        name="conv_mixer",
    )(proj, proj, proj, proj, proj, conv_w, row(conv_b), row(ln_g), row(ln_b))


def _qkv_kernel(q_ref, k_ref, cos_ref, sin_ref, qg_ref, kg_ref, ones_ref, qo_ref, ko_ref, *, q_scale):
    cos = cos_ref[...]
    sin = sin_ref[...]
    ones = ones_ref[...]
    lane = lax.broadcasted_iota(jnp.int32, cos.shape, 1)
    first_half = (lane % DA_HEAD_DIM) < (DA_HEAD_DIM // 2)

    def prep(x, g, scale):
        x2 = x * x
        hi = x2.astype(BF16)
        lo = (x2 - hi.astype(F32)).astype(BF16)
        ss = (jnp.dot(hi, ones, preferred_element_type=F32)
              + jnp.dot(lo, ones, preferred_element_type=F32))
        y = x * lax.rsqrt(ss * (1.0 / DA_HEAD_DIM) + EPS) * g
        partner = jnp.where(first_half, pltpu.roll(y, LANES - DA_HEAD_DIM // 2, 1),
                            pltpu.roll(y, DA_HEAD_DIM // 2, 1))
        return (y * cos + partner * sin) * scale

    for h in range(q_ref.shape[-1] // LANES):
        sl = slice(h * LANES, (h + 1) * LANES)
        qo_ref[:, sl] = prep(q_ref[:, sl].astype(F32), qg_ref[...], q_scale).astype(qo_ref.dtype)
        ko_ref[:, sl] = prep(k_ref[:, sl].astype(F32), kg_ref[...], 1.0).astype(ko_ref.dtype)


def _qkv_prep(proj, qn_g, kn_g, *, seq, tt, col0):
    t = proj.shape[0]
    w = DA_HEADS * 2 * DA_HEAD_DIM
    half = DA_HEAD_DIM // 2
    freqs = ROPE_THETA ** (-jnp.arange(half, dtype=F32) / half)
    ang = jnp.arange(seq, dtype=F32)[:, None] * freqs[None, :]
    cos = jnp.tile(jnp.cos(ang), (1, LANES // half))
    sin = jnp.tile(jnp.concatenate([-jnp.sin(ang), jnp.sin(ang)], axis=-1), (1, LANES // DA_HEAD_DIM))
    blk = jnp.arange(LANES) // DA_HEAD_DIM
    ones = (blk[:, None] == blk[None, :]).astype(BF16)
    tile_g = lambda g: jnp.tile(g.astype(F32), LANES // DA_HEAD_DIM).reshape(1, LANES)
    cb = col0 // w
    nseq = seq // tt
    out = jax.ShapeDtypeStruct((t, w), BF16)
    return pl.pallas_call(
        functools.partial(_qkv_kernel, q_scale=ATT_Q_SCALE),
        grid=(t // tt,),
        in_specs=[
            pl.BlockSpec((tt, w), lambda i: (i, cb)),
            pl.BlockSpec((tt, w), lambda i: (i, cb + 1)),
            pl.BlockSpec((tt, LANES), lambda i: (i % nseq, 0)),
            pl.BlockSpec((tt, LANES), lambda i: (i % nseq, 0)),
            pl.BlockSpec((1, LANES), lambda i: (0, 0)),
            pl.BlockSpec((1, LANES), lambda i: (0, 0)),
            pl.BlockSpec((LANES, LANES), lambda i: (0, 0)),
        ],
        out_specs=[pl.BlockSpec((tt, w), lambda i: (i, 0))] * 2,
        out_shape=[out, out],
        compiler_params=_params("parallel"),
        name="qkv_prep",
    )(proj, proj, cos, sin, tile_g(qn_g), tile_g(kn_g), ones)


def _attn_kernel(lam_ref, q_ref, k_ref, v_ref, bg_ref, sg_ref, o_ref,
                 qs_ref, vaug_ref, s0_ref, s1_ref, s2_ref, p0_ref, p1_ref, p2_ref, m_ref,
                 a0_ref, a1_ref, a2_ref, acc_ref, *, tq, post_scale):
    i = pl.program_id(2)
    s_refs, p_refs, a_refs = (s0_ref, s1_ref, s2_ref), (p0_ref, p1_ref, p2_ref), (a0_ref, a1_ref, a2_ref)

    @pl.when(i == 0)
    def _():
        vaug_ref[:, 0:LANES] = v_ref[...]
        vaug_ref[:, LANES:2 * LANES] = jnp.ones(v_ref.shape, vaug_ref.dtype)

    q = q_ref[...]
    lane = lax.broadcasted_iota(jnp.int32, q.shape, 1)
    zero = jnp.zeros_like(q)
    qs_ref[0:tq, :] = jnp.where(lane < DA_HEAD_DIM, q, zero)
    qs_ref[tq:2 * tq, :] = jnp.where(lane >= DA_HEAD_DIM, q, zero)
    acc_ref[...] = jnp.zeros(acc_ref.shape, F32)
    reps = tq // LANES
    bound = lam_ref[1]
    fixed_shift = bound <= ATT_FIXED_SHIFT_MAX

    def scores(j, slot):
        start = pl.multiple_of(j * tq, tq)
        s_refs[slot][...] = lax.dot_general(qs_ref[...], k_ref[pl.ds(start, tq), :],
                                            (((1,), (1,)), ((), ())), preferred_element_type=F32)

    def masked_scores(s_ref, rs, r0, masked):
        s = s_ref[rs, :]
        if masked:
            row = (r0 % tq) + lax.broadcasted_iota(jnp.int32, s.shape, 0)
            col = lax.broadcasted_iota(jnp.int32, s.shape, 1)
            s = jnp.where(col <= row, s, -jnp.inf)
        return s

    def softmax_online(slot, masked):
        s_ref, p_ref, a_ref = s_refs[slot], p_refs[slot], a_refs[slot]
        for r0 in range(0, 2 * tq, ATT_ROWS):
            rs = slice(r0, r0 + ATT_ROWS)
            s = masked_scores(s_ref, rs, r0, masked)
            m_prev = m_ref[rs, :]
            m_new = jnp.maximum(m_prev, jnp.max(s, axis=-1, keepdims=True))
            a_ref[rs, :] = jnp.exp2(m_prev - m_new)
            m_ref[rs, :] = m_new
            p_ref[rs, :] = jnp.exp2(s - jnp.concatenate([m_new] * reps, axis=1)).astype(p_ref.dtype)

    def accumulate_online(j, slot):
        start = pl.multiple_of(j * tq, tq)
        alpha = a_refs[slot][...]
        acc_ref[...] = jnp.concatenate([alpha, alpha], axis=1) * acc_ref[...] + jnp.dot(
            p_refs[slot][...], vaug_ref[pl.ds(start, tq), :], preferred_element_type=F32)

    def softmax_fixed(slot, masked):
        s_ref, p_ref = s_refs[slot], p_refs[slot]
        for r0 in range(0, 2 * tq, ATT_ROWS):
            rs = slice(r0, r0 + ATT_ROWS)
            s = masked_scores(s_ref, rs, r0, masked)
            p_ref[rs, :] = jnp.exp2(s - bound).astype(p_ref.dtype)

    def accumulate_fixed(j, slot):
        start = pl.multiple_of(j * tq, tq)
        acc_ref[...] = acc_ref[...] + jnp.dot(p_refs[slot][...], vaug_ref[pl.ds(start, tq), :],
                                              preferred_element_type=F32)

    def run(softmax, accumulate):
        scores(0, 0)

        @pl.when(i == 0)
        def _():
            softmax(0, True)
            accumulate(0, 0)

        @pl.when(i > 0)
        def _():
            scores(1, 1)
            softmax(0, False)
            npairs = (i - 1) // 2

            def pair(tt, carry):
                t = 2 * tt
                scores(t + 2, 0)
                softmax(1, False)
                accumulate(t, 0)
                scores(t + 3, 1)
                softmax(0, False)
                accumulate(t + 1, 1)
                return carry

            lax.fori_loop(0, npairs, pair, 0)

            @pl.when(i % 2 == 1)
            def _():
                softmax(1, True)
                accumulate(i - 1, 0)
                accumulate(i, 1)

            @pl.when(i % 2 == 0)
            def _():
                scores(i, 2)
                softmax(1, False)
                accumulate(i - 2, 0)
                softmax(2, True)
                accumulate(i - 1, 1)
                accumulate(i, 2)

    @pl.when(fixed_shift)
    def _():
        run(softmax_fixed, accumulate_fixed)

    @pl.when(jnp.logical_not(fixed_shift))
    def _():
        m_ref[...] = jnp.full(m_ref.shape, -jnp.inf, F32)
        run(softmax_online, accumulate_online)

    o = acc_ref[:, 0:LANES] / acc_ref[:, LANES:2 * LANES]
    d = o[:tq] - lam_ref[0] * o[tq:]
    ms = jnp.mean(d * d, axis=-1, keepdims=True)
    y = d * lax.rsqrt(ms + EPS) * sg_ref[...] * post_scale
    o_ref[...] = (y * _silu(bg_ref[...].astype(F32))).astype(o_ref.dtype)


def _diff_attention(lam, score_bound, qn, kn, proj, subln_g, *, batch, seq, tq, v_col0, gate_col0, post_scale):
    t, w = qn.shape
    nq = seq // tq
    vcb = v_col0 // LANES
    gcb = gate_col0 // LANES
    stat = pltpu.VMEM((2 * tq, LANES), F32)
    sbuf = pltpu.VMEM((2 * tq, tq), F32)
    pbuf = pltpu.VMEM((2 * tq, tq), BF16)
    return pl.pallas_call(
        functools.partial(_attn_kernel, tq=tq, post_scale=post_scale),
        grid=(batch, DA_HEADS, nq),
        in_specs=[
            pl.BlockSpec(memory_space=pltpu.SMEM),
            pl.BlockSpec((tq, LANES), lambda b, h, i: (b * nq + i, h)),
            pl.BlockSpec((seq, LANES), lambda b, h, i: (b, h)),
            pl.BlockSpec((seq, LANES), lambda b, h, i: (b, vcb + h)),
            pl.BlockSpec((tq, LANES), lambda b, h, i: (b * nq + i, gcb + h)),
            pl.BlockSpec((1, LANES), lambda b, h, i: (0, 0)),
        ],
        out_specs=pl.BlockSpec((tq, LANES), lambda b, h, i: (b * nq + i, h)),
        out_shape=jax.ShapeDtypeStruct((t, w), BF16),
        scratch_shapes=[pltpu.VMEM((2 * tq, LANES), BF16), pltpu.VMEM((seq, 2 * LANES), BF16),
                        sbuf, sbuf, sbuf, pbuf, pbuf, pbuf, stat, stat, stat, stat,
                        pltpu.VMEM((2 * tq, 2 * LANES), F32)],
        compiler_params=_params("parallel", "parallel", "arbitrary"),
        name="diff_attention",
    )(jnp.stack([lam, score_bound]).astype(F32), qn, kn, proj, proj, subln_g.astype(F32).reshape(1, LANES))


def _proj_res_kernel(a_ref, b_ref, w_ref, x_ref, o_ref):
    kh = a_ref.shape[1]
    half = o_ref.shape[0] // 2
    for r0 in (0, half):
        rs = slice(r0, r0 + half)
        y = (jnp.dot(a_ref[rs, :], w_ref[0:kh, :], preferred_element_type=F32)
             + jnp.dot(b_ref[rs, :], w_ref[kh:2 * kh, :], preferred_element_type=F32))
        o_ref[rs, :] = x_ref[rs, :] + y


def _proj_residual(a, b, w, x, *, tm):
    m, n = x.shape
    kh = w.shape[0] // 2
    return pl.pallas_call(
        _proj_res_kernel,
        grid=(m // tm,),
        in_specs=[
            pl.BlockSpec((tm, kh), lambda i: (i, 0)),
            pl.BlockSpec((tm, kh), lambda i: (i, 0)),
            pl.BlockSpec(w.shape, lambda i: (0, 0), pipeline_mode=pl.Buffered(1)),
            pl.BlockSpec((tm, n), lambda i: (i, 0)),
        ],
        out_specs=pl.BlockSpec((tm, n), lambda i: (i, 0)),
        out_shape=jax.ShapeDtypeStruct((m, n), F32),
        compiler_params=_params("parallel"),
        name="proj_residual",
    )(a, b, w, x)


def _split_bf16(x):
    hi = x.astype(BF16)
    return hi, (x - hi.astype(F32)).astype(BF16)


def _expand_lanes(x, onehot):
    hi, lo = _split_bf16(x)
    return (jnp.dot(hi, onehot, preferred_element_type=F32) + jnp.dot(lo, onehot, preferred_element_type=F32))


def _s5_kernel(u_ref, d_ref, pvr_ref, pvi_ref, bbr_ref, bbi_ref, cc_ref, cc2_ref, prx_ref, pix_ref,
               lr_ref, li_ref, ohl_ref, ohm_ref, y_ref, ut_ref, yt_ref, toep_ref, cw_ref,
               *, chunks_per_seq):
    p = S5_STATE
    gb, width, ncol = ut_ref.shape
    mm = cc_ref.shape[1]
    nl = width // mm
    nc = chunks_per_seq
    reps = ncol // LANES
    lane = lax.broadcasted_iota(jnp.int32, (p, ncol), 1)
    cpos = lane % nc

    def shifted(x, k):
        return jnp.where(cpos >= k, pltpu.roll(x, k, 1), 0.0)

    for b in range(ncol // nc):
        for l in range(nl):
            xt = u_ref[(b * nl + l) * nc:(b * nl + l + 1) * nc, :].astype(F32).T.astype(ut_ref.dtype)
            for g in range(gb):
                ut_ref[g, l * mm:(l + 1) * mm, b * nc:(b + 1) * nc] = xt[g * mm:(g + 1) * mm, :]

    for g in range(gb):
        u = ut_ref[g]
        prt, pit = _expand_lanes(pvr_ref[g], ohl_ref[...]), _expand_lanes(pvi_ref[g], ohl_ref[...])
        bbr, bbi = _expand_lanes(bbr_ref[g], ohm_ref[...]), _expand_lanes(bbi_ref[g], ohm_ref[...])
        sw_hi, sw_lo = _split_bf16(jnp.concatenate([prt * bbr - pit * bbi, prt * bbi + pit * bbr], axis=0))
        c_hi, c_lo = _split_bf16(cc_ref[g])
        for l in range(nl):
            cw_ref[l * mm:(l + 1) * mm, :] = (cc_ref[g] * prx_ref[g, l:l + 1, :]
                                              + cc2_ref[g] * pix_ref[g, l:l + 1, :]).astype(cw_ref.dtype)
        kcat = (jnp.dot(c_hi, sw_hi, preferred_element_type=F32)
                + jnp.dot(c_hi, sw_lo, preferred_element_type=F32)
                + jnp.dot(c_lo, sw_hi, preferred_element_type=F32))
        z = jnp.concatenate([kcat, jnp.zeros_like(kcat)], axis=1)
        rolled = [z if r == 0 else pltpu.roll(z, 2 * width - r, 1) for r in range(0, LANES, mm)]
        for l in range(nl):
            off = (nl - 1 - l) * mm
            base = off - off % LANES
            strip = rolled[(off % LANES) // mm][:, base:base + width]
            toep_ref[l * mm:(l + 1) * mm, :] = strip.astype(toep_ref.dtype)
        s = jnp.dot(sw_hi, u, preferred_element_type=F32)
        xr, xi = s[:p], s[p:]
        lr = jnp.concatenate([lr_ref[g]] * reps, axis=1)
        li = jnp.concatenate([li_ref[g]] * reps, axis=1)
        k = 1
        while k < chunks_per_seq:
            sr, si = shifted(xr, k), shifted(xi, k)
            xr, xi = xr + (lr * sr - li * si), xi + (lr * si + li * sr)
            lr, li = lr * lr - li * li, 2.0 * (lr * li)
            k *= 2
        h = jnp.concatenate([shifted(xr, 1), shifted(xi, 1)], axis=0).astype(BF16)
        yt_ref[g] = (jnp.dot(toep_ref[...], u, preferred_element_type=F32)
                     + jnp.dot(cw_ref[...], h, preferred_element_type=F32))

    for b in range(ncol // nc):
        for l in range(nl):
            rows = slice((b * nl + l) * nc, (b * nl + l + 1) * nc)
            zt = jnp.concatenate([yt_ref[g, l * mm:(l + 1) * mm, b * nc:(b + 1) * nc] for g in range(gb)], axis=0)
            y = zt.T + d_ref[...] * u_ref[rows, :].astype(F32)
            y_ref[rows, :] = _gelu_tanh(y).astype(y_ref.dtype)


def _s5_tables(a_re, a_im, log_dt, b_re, b_im, c_re, c_im, chunk):
    g, p, m = b_re.shape
    dt = jnp.exp(log_dt.astype(F32))[:, None]
    a_re, a_im = a_re.astype(F32), a_im.astype(F32)
    mag = jnp.exp(a_re * dt)
    lb_re, lb_im = mag * jnp.cos(a_im * dt), mag * jnp.sin(a_im * dt)
    den = a_re * a_re + a_im * a_im
    nr, ni = lb_re - 1.0, lb_im
    fr = (nr * a_re + ni * a_im) / den
    fi = (ni * a_re - nr * a_im) / den
    bb_re = fr[..., None] * b_re - fi[..., None] * b_im
    bb_im = fr[..., None] * b_im + fi[..., None] * b_re
    d = jnp.arange(chunk + 1, dtype=F32)[None, :, None]
    pmag = jnp.exp((a_re * dt)[:, None, :] * d)
    pang = (a_im * dt)[:, None, :] * d
    pr, pi = pmag * jnp.cos(pang), pmag * jnp.sin(pang)
    c_re, c_im = c_re.astype(F32), c_im.astype(F32)
    pad = lambda a: jnp.pad(a, ((0, 0), (0, 0), (0, LANES - a.shape[2])))
    rev = lambda a: pad(jnp.flip(a[:, :chunk], axis=1).transpose(0, 2, 1))
    onehot_l = jnp.repeat(jnp.eye(LANES, chunk, dtype=BF16), m, axis=1)
    onehot_m = jnp.tile(jnp.eye(LANES, m, dtype=BF16), (1, chunk))
    cc = jnp.concatenate([c_re, -c_im], axis=-1)
    cc2 = jnp.concatenate([-c_im, -c_re], axis=-1)
    prx = jnp.concatenate([pr[:, 1:], pr[:, 1:]], axis=-1)
    pix = jnp.concatenate([pi[:, 1:], pi[:, 1:]], axis=-1)
    lam_r = jnp.broadcast_to(pr[:, chunk, :, None], (g, p, LANES))
    lam_i = jnp.broadcast_to(pi[:, chunk, :, None], (g, p, LANES))
    return (rev(pr), rev(pi), pad(bb_re), pad(bb_im), cc, cc2, prx, pix, lam_r, lam_i), (onehot_l, onehot_m)


def _s5(proj, d_skip, tables, *, batch, seq):
    per_group, onehots = tables
    cc, prx = per_group[4], per_group[6]
    t = proj.shape[0]
    g, m = cc.shape[0], cc.shape[1]
    e = g * m
    chunk = prx.shape[1]
    nc = seq // chunk
    ncol = t // chunk
    gb = LANES // m
    up = proj.reshape(batch, nc, chunk, proj.shape[1])[..., :e].transpose(0, 2, 1, 3).reshape(t, e)
    wspec = lambda a: pl.BlockSpec((gb,) + a.shape[1:], lambda i: (i, 0, 0))
    const = lambda a: pl.BlockSpec(a.shape, lambda i: (0, 0))
    yp = pl.pallas_call(
        functools.partial(_s5_kernel, chunks_per_seq=nc),
        grid=(g // gb,),
        in_specs=([pl.BlockSpec((t, LANES), lambda i: (0, i)), pl.BlockSpec((1, LANES), lambda i: (0, i))]
                  + [wspec(a) for a in per_group] + [const(a) for a in onehots]),
        out_specs=pl.BlockSpec((t, LANES), lambda i: (0, i)),
        out_shape=jax.ShapeDtypeStruct((t, e), BF16),
        scratch_shapes=[pltpu.VMEM((gb, chunk * m, ncol), BF16), pltpu.VMEM((gb, chunk * m, ncol), F32),
                        pltpu.VMEM((chunk * m, chunk * m), BF16), pltpu.VMEM((chunk * m, 2 * S5_STATE), BF16)],
        compiler_params=_params("parallel"),
        name="s5_chunked",
    )(up, d_skip.astype(F32).reshape(1, e), *per_group, *onehots)
    return yp.reshape(batch, chunk, nc, e).transpose(0, 2, 1, 3).reshape(t, e)


def _glu_out_kernel(z_ref, gate_ref, wg_ref, b_ref, wo_ref, x_ref, o_ref):
    half = o_ref.shape[0] // 2
    for r0 in (0, half):
        rs = slice(r0, r0 + half)
        z = z_ref[rs, :]
        t = jnp.dot(z, wg_ref[...], preferred_element_type=F32) + b_ref[...]
        out = z.astype(F32) * jax.nn.sigmoid(t) * _silu(gate_ref[rs, :].astype(F32))
        o_ref[rs, :] = x_ref[rs, :] + jnp.dot(out.astype(BF16), wo_ref[...], preferred_element_type=F32)


def _glu_out(z, proj, w_glu, b_glu, w_out, x, *, tm):
    t, e = z.shape
    n = w_out.shape[1]
    resident = lambda a: pl.BlockSpec(a.shape, lambda i: (0, 0), pipeline_mode=pl.Buffered(1))
    return pl.pallas_call(
        _glu_out_kernel,
        grid=(t // tm,),
        in_specs=[
            pl.BlockSpec((tm, e), lambda i: (i, 0)),
            pl.BlockSpec((tm, e), lambda i: (i, 1)),
            resident(w_glu),
            pl.BlockSpec((1, e), lambda i: (0, 0)),
            resident(w_out),
            pl.BlockSpec((tm, n), lambda i: (i, 0)),
        ],
        out_specs=pl.BlockSpec((tm, n), lambda i: (i, 0)),
        out_shape=jax.ShapeDtypeStruct((t, n), F32),
        compiler_params=_params("parallel"),
        name="glu_out_residual",
    )(z, proj, w_glu, b_glu.astype(F32).reshape(1, e), w_out, x)


def _even_layer(x, layer_idx, norm_g, w_in, conv_w, conv_b, cln_g, cln_b, qn_g, kn_g,
                lam_q1, lam_k1, lam_q2, lam_k2, subln_g, w_out, *, batch, seq):
    conv_ch = conv_w.shape[1]
    proj = _rms_matmul(x, norm_g, w_in.astype(BF16), tm=512, nsplit=2)
    mix_a = _conv_mixer(proj, conv_w.astype(F32), conv_b.astype(F32), cln_g.astype(F32),
                        cln_b.astype(F32), seq=seq, tt=256)
    qn, kn = _qkv_prep(proj, qn_g, kn_g, seq=seq, tt=512, col0=3 * conv_ch)
    lam_init = 0.8 - 0.6 * math.exp(-0.3 * layer_idx)
    lam = (jnp.exp(jnp.sum(lam_q1.astype(F32) * lam_k1.astype(F32)))
           - jnp.exp(jnp.sum(lam_q2.astype(F32) * lam_k2.astype(F32))) + lam_init)
    qkv_w = qn.shape[1]
    score_bound = (DA_HEAD_DIM * ATT_Q_SCALE * 1.02 * jnp.max(jnp.abs(qn_g.astype(F32)))
                   * jnp.max(jnp.abs(kn_g.astype(F32))) + 0.5)
    mix_b = _diff_attention(lam, score_bound, qn, kn, proj, subln_g, batch=batch, seq=seq, tq=512,
                            v_col0=3 * conv_ch + 2 * qkv_w, gate_col0=3 * conv_ch + 3 * qkv_w,
                            post_scale=1.0 - lam_init)
    return _proj_residual(mix_a, mix_b, w_out.astype(BF16), x, tm=512)


def _odd_layer(x, norm_g, w_in, a_re, a_im, log_dt, b_re, b_im, c_re, c_im, d_skip,
               w_glu, b_glu, w_out, *, batch, seq):
    proj = _rms_matmul(x, norm_g, w_in.astype(BF16), tm=512, nsplit=1)
    tables = _s5_tables(a_re, a_im, log_dt, b_re, b_im, c_re, c_im, S5_CHUNK)
    z = _s5(proj, d_skip, tables, batch=batch, seq=seq)
    return _glu_out(z, proj, w_glu.astype(BF16), b_glu, w_out.astype(BF16), x, tm=512)


def kernel(x, e_norm_g, e_w_in, e_conv_w, e_conv_b, e_cln_g, e_cln_b, e_qn_g, e_kn_g, e_lam_q1, e_lam_k1, e_lam_q2, e_lam_k2, e_subln_g, e_w_out, o_norm_g, o_w_in, o_A_re, o_A_im, o_log_dt, o_B_re, o_B_im, o_C_re, o_C_im, o_D, o_w_glu, o_b_glu, o_w_out):
    batch, seq, d_model = x.shape
    depth = e_norm_g.shape[0] + o_norm_g.shape[0]
    h = x.reshape(batch * seq, d_model)
    for layer in range(depth):
        j = layer // 2
        if layer % 2 == 0:
            h = _even_layer(h, layer, e_norm_g[j], e_w_in[j], e_conv_w[j], e_conv_b[j], e_cln_g[j],
                            e_cln_b[j], e_qn_g[j], e_kn_g[j], e_lam_q1[j], e_lam_k1[j], e_lam_q2[j],
                            e_lam_k2[j], e_subln_g[j], e_w_out[j], batch=batch, seq=seq)
        else:
            h = _odd_layer(h, o_norm_g[j], o_w_in[j], o_A_re[j], o_A_im[j], o_log_dt[j], o_B_re[j],
                           o_B_im[j], o_C_re[j], o_C_im[j], o_D[j], o_w_glu[j], o_b_glu[j], o_w_out[j],
                           batch=batch, seq=seq)
    return h.reshape(batch, seq, d_model)
```

```python
import functools
import math

import jax
import jax.numpy as jnp
from jax import lax
from jax.experimental import pallas as pl
from jax.experimental.pallas import tpu as pltpu

F32 = jnp.float32
BF16 = jnp.bfloat16

EPS = 1e-6
ROPE_THETA = 10000.0
CONV_W = 31
DA_HEADS = 8
DA_HEAD_DIM = 64
S5_GROUP = 16
S5_STATE = 64

LANES = 128
CONV_HALO = 32
CONV_ROWS = 32
ATT_ROWS = 64
ATT_FIXED_SHIFT_MAX = 40.0
ATT_Q_SCALE = math.log2(math.e) * DA_HEAD_DIM ** -0.5
S5_CHUNK = 32
VMEM_LIMIT = 56 * 1024 * 1024


def _params(*sem):
    return pltpu.CompilerParams(dimension_semantics=sem, vmem_limit_bytes=VMEM_LIMIT)


def _silu(x):
    return x * jax.nn.sigmoid(x)


def _gelu_tanh(x):
    return 0.5 * x * (1.0 + jnp.tanh(math.sqrt(2.0 / math.pi) * (x + 0.044715 * (x * x * x))))


def _rms_matmul_kernel(x_ref, g_ref, w_ref, o_ref):
    half = o_ref.shape[0] // 2
    for r0 in (0, half):
        rs = slice(r0, r0 + half)
        x = x_ref[rs, :]
        ms = jnp.mean(x * x, axis=-1, keepdims=True)
        h = (x * lax.rsqrt(ms + EPS) * g_ref[...]).astype(BF16)
        o_ref[rs, :] = jnp.dot(h, w_ref[...], preferred_element_type=F32).astype(o_ref.dtype)


def _rms_matmul(x, g, w, *, tm, nsplit):
    m, d = x.shape
    n = w.shape[1]
    tn = n // nsplit
    return pl.pallas_call(
        _rms_matmul_kernel,
        grid=(nsplit, m // tm),
        in_specs=[
            pl.BlockSpec((tm, d), lambda j, i: (i, 0)),
            pl.BlockSpec((1, d), lambda j, i: (0, 0)),
            pl.BlockSpec((d, tn), lambda j, i: (0, j), pipeline_mode=pl.Buffered(1)),
        ],
        out_specs=pl.BlockSpec((tm, tn), lambda j, i: (i, j)),
        out_shape=jax.ShapeDtypeStruct((m, n), BF16),
        compiler_params=_params("parallel", "parallel"),
        name="rms_matmul",
    )(x, g.reshape(1, d), w)


def _conv_kernel(val_ref, glu_ref, gate_ref, hval_ref, hglu_ref, w_ref, cb_ref, lg_ref, lb_ref,
                 o_ref, uext_ref, *, tt, tiles_per_seq):
    c = o_ref.shape[-1]
    nlt = c // LANES
    first = (pl.program_id(0) % tiles_per_seq) == 0
    f32 = lambda ref, idx=Ellipsis: ref[idx].astype(F32)
    halo = jnp.where(first, 0.0, f32(hval_ref) * jax.nn.sigmoid(f32(hglu_ref)))
    cur = f32(val_ref) * jax.nn.sigmoid(f32(glu_ref))
    for j in range(nlt):
        sl = slice(j * LANES, (j + 1) * LANES)
        uext_ref[j, pl.ds(0, CONV_HALO, stride=2), :] = halo[:, sl]
        uext_ref[j, pl.ds(2 * CONV_HALO, tt, stride=2), :] = cur[:, sl]
    lg = lg_ref[...]
    lb = lb_ref[...]

    def chunk(r, carry):
        base = pl.multiple_of(r * CONV_ROWS, CONV_ROWS)
        accs = []
        for j in range(nlt):
            sl = slice(j * LANES, (j + 1) * LANES)
            acc = jnp.broadcast_to(cb_ref[:, sl], (CONV_ROWS, LANES))
            for k in range(CONV_W):
                row = base + (CONV_HALO - CONV_W + 1 + k)
                acc = acc + w_ref[k:k + 1, sl] * uext_ref[j, pl.ds(2 * row, CONV_ROWS, stride=2), :]
            accs.append(acc)
        acc = jnp.concatenate(accs, axis=1)
        mean = jnp.mean(acc, axis=-1, keepdims=True)
        xc = acc - mean
        var = jnp.mean(xc * xc, axis=-1, keepdims=True)
        y = _silu(xc * lax.rsqrt(var + EPS) * lg + lb)
        gate = gate_ref[pl.ds(base, CONV_ROWS), :].astype(F32)
        o_ref[pl.ds(base, CONV_ROWS), :] = (y * _silu(gate)).astype(o_ref.dtype)
        return carry

    lax.fori_loop(0, tt // CONV_ROWS, chunk, 0, unroll=2)


def _conv_mixer(proj, conv_w, conv_b, ln_g, ln_b, *, seq, tt):
    t = proj.shape[0]
    c = conv_w.shape[1]
    hb = tt // CONV_HALO
    row = lambda a: a.reshape(1, c)
    return pl.pallas_call(
        functools.partial(_conv_kernel, tt=tt, tiles_per_seq=seq // tt),
        grid=(t // tt,),
        in_specs=[
            pl.BlockSpec((tt, c), lambda i: (i, 0)),
            pl.BlockSpec((tt, c), lambda i: (i, 1)),
            pl.BlockSpec((tt, c), lambda i: (i, 2)),
            pl.BlockSpec((CONV_HALO, c), lambda i: (jnp.maximum(i * hb - 1, 0), 0)),
            pl.BlockSpec((CONV_HALO, c), lambda i: (jnp.maximum(i * hb - 1, 0), 1)),
            pl.BlockSpec((CONV_W, c), lambda i: (0, 0)),
            pl.BlockSpec((1, c), lambda i: (0, 0)),
            pl.BlockSpec((1, c), lambda i: (0, 0)),
            pl.BlockSpec((1, c), lambda i: (0, 0)),
        ],
        out_specs=pl.BlockSpec((tt, c), lambda i: (i, 0)),
        out_shape=jax.ShapeDtypeStruct((t, c), BF16),
        scratch_shapes=[pltpu.VMEM((c // LANES, 2 * (CONV_HALO + tt), LANES), F32)],
        compiler_params=_params("parallel"),
        name="conv_mixer",
    )(proj, proj, proj, proj, proj, conv_w, row(conv_b), row(ln_g), row(ln_b))


def _qkv_kernel(q_ref, k_ref, cos_ref, sin_ref, qg_ref, kg_ref, ones_ref, qo_ref, ko_ref, *, q_scale):
    cos = cos_ref[...]
    sin = sin_ref[...]
    ones = ones_ref[...]
    lane = lax.broadcasted_iota(jnp.int32, cos.shape, 1)
    first_half = (lane % DA_HEAD_DIM) < (DA_HEAD_DIM // 2)

    def prep(x, g, scale):
        x2 = x * x
        hi = x2.astype(BF16)
        lo = (x2 - hi.astype(F32)).astype(BF16)
        ss = (jnp.dot(hi, ones, preferred_element_type=F32)
              + jnp.dot(lo, ones, preferred_element_type=F32))
        y = x * lax.rsqrt(ss * (1.0 / DA_HEAD_DIM) + EPS) * g
        partner = jnp.where(first_half, pltpu.roll(y, LANES - DA_HEAD_DIM // 2, 1),
                            pltpu.roll(y, DA_HEAD_DIM // 2, 1))
        return (y * cos + partner * sin) * scale

    for h in range(q_ref.shape[-1] // LANES):
        sl = slice(h * LANES, (h + 1) * LANES)
        qo_ref[:, sl] = prep(q_ref[:, sl].astype(F32), qg_ref[...], q_scale).astype(qo_ref.dtype)
        ko_ref[:, sl] = prep(k_ref[:, sl].astype(F32), kg_ref[...], 1.0).astype(ko_ref.dtype)


def _qkv_prep(proj, qn_g, kn_g, *, seq, tt, col0):
    t = proj.shape[0]
    w = DA_HEADS * 2 * DA_HEAD_DIM
    half = DA_HEAD_DIM // 2
    freqs = ROPE_THETA ** (-jnp.arange(half, dtype=F32) / half)
    ang = jnp.arange(seq, dtype=F32)[:, None] * freqs[None, :]
    cos = jnp.tile(jnp.cos(ang), (1, LANES // half))
    sin = jnp.tile(jnp.concatenate([-jnp.sin(ang), jnp.sin(ang)], axis=-1), (1, LANES // DA_HEAD_DIM))
    blk = jnp.arange(LANES) // DA_HEAD_DIM
    ones = (blk[:, None] == blk[None, :]).astype(BF16)
    tile_g = lambda g: jnp.tile(g.astype(F32), LANES // DA_HEAD_DIM).reshape(1, LANES)
    cb = col0 // w
    nseq = seq // tt
    out = jax.ShapeDtypeStruct((t, w), BF16)
    return pl.pallas_call(
        functools.partial(_qkv_kernel, q_scale=ATT_Q_SCALE),
        grid=(t // tt,),
        in_specs=[
            pl.BlockSpec((tt, w), lambda i: (i, cb)),
            pl.BlockSpec((tt, w), lambda i: (i, cb + 1)),
            pl.BlockSpec((tt, LANES), lambda i: (i % nseq, 0)),
            pl.BlockSpec((tt, LANES), lambda i: (i % nseq, 0)),
            pl.BlockSpec((1, LANES), lambda i: (0, 0)),
            pl.BlockSpec((1, LANES), lambda i: (0, 0)),
            pl.BlockSpec((LANES, LANES), lambda i: (0, 0)),
        ],
        out_specs=[pl.BlockSpec((tt, w), lambda i: (i, 0))] * 2,
        out_shape=[out, out],
        compiler_params=_params("parallel"),
        name="qkv_prep",
    )(proj, proj, cos, sin, tile_g(qn_g), tile_g(kn_g), ones)


def _attn_kernel(lam_ref, q_ref, k_ref, v_ref, bg_ref, sg_ref, o_ref,
                 qs_ref, vaug_ref, s0_ref, s1_ref, s2_ref, p0_ref, p1_ref, p2_ref, m_ref,
                 a0_ref, a1_ref, a2_ref, acc_ref, *, tq, post_scale):
    i = pl.program_id(2)
    s_refs, p_refs, a_refs = (s0_ref, s1_ref, s2_ref), (p0_ref, p1_ref, p2_ref), (a0_ref, a1_ref, a2_ref)

    @pl.when(i == 0)
    def _():
        vaug_ref[:, 0:LANES] = v_ref[...]
        vaug_ref[:, LANES:2 * LANES] = jnp.ones(v_ref.shape, vaug_ref.dtype)

    q = q_ref[...]
    lane = lax.broadcasted_iota(jnp.int32, q.shape, 1)
    zero = jnp.zeros_like(q)
    qs_ref[0:tq, :] = jnp.where(lane < DA_HEAD_DIM, q, zero)
    qs_ref[tq:2 * tq, :] = jnp.where(lane >= DA_HEAD_DIM, q, zero)
    acc_ref[...] = jnp.zeros(acc_ref.shape, F32)
    reps = tq // LANES
    bound = lam_ref[1]
    fixed_shift = bound <= ATT_FIXED_SHIFT_MAX

    def scores(j, slot):
        start = pl.multiple_of(j * tq, tq)
        s_refs[slot][...] = lax.dot_general(qs_ref[...], k_ref[pl.ds(start, tq), :],
                                            (((1,), (1,)), ((), ())), preferred_element_type=F32)

    def masked_scores(s_ref, rs, r0, masked):
        s = s_ref[rs, :]
        if masked:
            row = (r0 % tq) + lax.broadcasted_iota(jnp.int32, s.shape, 0)
            col = lax.broadcasted_iota(jnp.int32, s.shape, 1)
            s = jnp.where(col <= row, s, -jnp.inf)
        return s

    def softmax_online(slot, masked):
        s_ref, p_ref, a_ref = s_refs[slot], p_refs[slot], a_refs[slot]
        for r0 in range(0, 2 * tq, ATT_ROWS):
            rs = slice(r0, r0 + ATT_ROWS)
            s = masked_scores(s_ref, rs, r0, masked)
            m_prev = m_ref[rs, :]
            m_new = jnp.maximum(m_prev, jnp.max(s, axis=-1, keepdims=True))
            a_ref[rs, :] = jnp.exp2(m_prev - m_new)
            m_ref[rs, :] = m_new
            p_ref[rs, :] = jnp.exp2(s - jnp.concatenate([m_new] * reps, axis=1)).astype(p_ref.dtype)

    def accumulate_online(j, slot):
        start = pl.multiple_of(j * tq, tq)
        alpha = a_refs[slot][...]
        acc_ref[...] = jnp.concatenate([alpha, alpha], axis=1) * acc_ref[...] + jnp.dot(
            p_refs[slot][...], vaug_ref[pl.ds(start, tq), :], preferred_element_type=F32)

    def softmax_fixed(slot, masked):
        s_ref, p_ref = s_refs[slot], p_refs[slot]
        for r0 in range(0, 2 * tq, ATT_ROWS):
            rs = slice(r0, r0 + ATT_ROWS)
            s = masked_scores(s_ref, rs, r0, masked)
            p_ref[rs, :] = jnp.exp2(s - bound).astype(p_ref.dtype)

    def accumulate_fixed(j, slot):
        start = pl.multiple_of(j * tq, tq)
        acc_ref[...] = acc_ref[...] + jnp.dot(p_refs[slot][...], vaug_ref[pl.ds(start, tq), :],
                                              preferred_element_type=F32)

    def run(softmax, accumulate):
        scores(0, 0)

        @pl.when(i == 0)
        def _():
            softmax(0, True)
            accumulate(0, 0)

        @pl.when(i > 0)
        def _():
            scores(1, 1)
            softmax(0, False)
            npairs = (i - 1) // 2

            def pair(tt, carry):
                t = 2 * tt
                scores(t + 2, 0)
                softmax(1, False)
                accumulate(t, 0)
                scores(t + 3, 1)
                softmax(0, False)
                accumulate(t + 1, 1)
                return carry

            lax.fori_loop(0, npairs, pair, 0)

            @pl.when(i % 2 == 1)
            def _():
                softmax(1, True)
                accumulate(i - 1, 0)
                accumulate(i, 1)

            @pl.when(i % 2 == 0)
            def _():
                scores(i, 2)
                softmax(1, False)
                accumulate(i - 2, 0)
                softmax(2, True)
                accumulate(i - 1, 1)
                accumulate(i, 2)

    @pl.when(fixed_shift)
    def _():
        run(softmax_fixed, accumulate_fixed)

    @pl.when(jnp.logical_not(fixed_shift))
    def _():
        m_ref[...] = jnp.full(m_ref.shape, -jnp.inf, F32)
        run(softmax_online, accumulate_online)

    o = acc_ref[:, 0:LANES] / acc_ref[:, LANES:2 * LANES]
    d = o[:tq] - lam_ref[0] * o[tq:]
    ms = jnp.mean(d * d, axis=-1, keepdims=True)
    y = d * lax.rsqrt(ms + EPS) * sg_ref[...] * post_scale
    o_ref[...] = (y * _silu(bg_ref[...].astype(F32))).astype(o_ref.dtype)


def _diff_attention(lam, score_bound, qn, kn, proj, subln_g, *, batch, seq, tq, v_col0, gate_col0, post_scale):
    t, w = qn.shape
    nq = seq // tq
    vcb = v_col0 // LANES
    gcb = gate_col0 // LANES
    stat = pltpu.VMEM((2 * tq, LANES), F32)
    sbuf = pltpu.VMEM((2 * tq, tq), F32)
    pbuf = pltpu.VMEM((2 * tq, tq), BF16)
    return pl.pallas_call(
        functools.partial(_attn_kernel, tq=tq, post_scale=post_scale),
        grid=(batch, DA_HEADS, nq),
        in_specs=[
            pl.BlockSpec(memory_space=pltpu.SMEM),
            pl.BlockSpec((tq, LANES), lambda b, h, i: (b * nq + i, h)),
            pl.BlockSpec((seq, LANES), lambda b, h, i: (b, h)),
            pl.BlockSpec((seq, LANES), lambda b, h, i: (b, vcb + h)),
            pl.BlockSpec((tq, LANES), lambda b, h, i: (b * nq + i, gcb + h)),
            pl.BlockSpec((1, LANES), lambda b, h, i: (0, 0)),
        ],
        out_specs=pl.BlockSpec((tq, LANES), lambda b, h, i: (b * nq + i, h)),
        out_shape=jax.ShapeDtypeStruct((t, w), BF16),
        scratch_shapes=[pltpu.VMEM((2 * tq, LANES), BF16), pltpu.VMEM((seq, 2 * LANES), BF16),
                        sbuf, sbuf, sbuf, pbuf, pbuf, pbuf, stat, stat, stat, stat,
                        pltpu.VMEM((2 * tq, 2 * LANES), F32)],
        compiler_params=_params("parallel", "parallel", "arbitrary"),
        name="diff_attention",
    )(jnp.stack([lam, score_bound]).astype(F32), qn, kn, proj, proj, subln_g.astype(F32).reshape(1, LANES))


def _proj_res_kernel(a_ref, b_ref, w_ref, x_ref, o_ref):
    kh = a_ref.shape[1]
    half = o_ref.shape[0] // 2
    for r0 in (0, half):
        rs = slice(r0, r0 + half)
        y = (jnp.dot(a_ref[rs, :], w_ref[0:kh, :], preferred_element_type=F32)
             + jnp.dot(b_ref[rs, :], w_ref[kh:2 * kh, :], preferred_element_type=F32))
        o_ref[rs, :] = x_ref[rs, :] + y


def _proj_residual(a, b, w, x, *, tm):
    m, n = x.shape
    kh = w.shape[0] // 2
    return pl.pallas_call(
        _proj_res_kernel,
        grid=(m // tm,),
        in_specs=[
            pl.BlockSpec((tm, kh), lambda i: (i, 0)),
            pl.BlockSpec((tm, kh), lambda i: (i, 0)),
            pl.BlockSpec(w.shape, lambda i: (0, 0), pipeline_mode=pl.Buffered(1)),
            pl.BlockSpec((tm, n), lambda i: (i, 0)),
        ],
        out_specs=pl.BlockSpec((tm, n), lambda i: (i, 0)),
        out_shape=jax.ShapeDtypeStruct((m, n), F32),
        compiler_params=_params("parallel"),
        name="proj_residual",
    )(a, b, w, x)


def _split_bf16(x):
    hi = x.astype(BF16)
    return hi, (x - hi.astype(F32)).astype(BF16)


def _expand_lanes(x, onehot):
    hi, lo = _split_bf16(x)
    return (jnp.dot(hi, onehot, preferred_element_type=F32) + jnp.dot(lo, onehot, preferred_element_type=F32))


def _s5_kernel(u_ref, d_ref, pvr_ref, pvi_ref, bbr_ref, bbi_ref, cc_ref, cc2_ref, prx_ref, pix_ref,
               lr_ref, li_ref, ohl_ref, ohm_ref, y_ref, ut_ref, yt_ref, toep_ref, cw_ref,
               *, chunks_per_seq):
    p = S5_STATE
    gb, width, ncol = ut_ref.shape
    mm = cc_ref.shape[1]
    nl = width // mm
    nc = chunks_per_seq
    reps = ncol // LANES
    lane = lax.broadcasted_iota(jnp.int32, (p, ncol), 1)
    cpos = lane % nc

    def shifted(x, k):
        return jnp.where(cpos >= k, pltpu.roll(x, k, 1), 0.0)

    for b in range(ncol // nc):
        for l in range(nl):
            xt = u_ref[(b * nl + l) * nc:(b * nl + l + 1) * nc, :].astype(F32).T.astype(ut_ref.dtype)
            for g in range(gb):
                ut_ref[g, l * mm:(l + 1) * mm, b * nc:(b + 1) * nc] = xt[g * mm:(g + 1) * mm, :]

    for g in range(gb):
        u = ut_ref[g]
        prt, pit = _expand_lanes(pvr_ref[g], ohl_ref[...]), _expand_lanes(pvi_ref[g], ohl_ref[...])
        bbr, bbi = _expand_lanes(bbr_ref[g], ohm_ref[...]), _expand_lanes(bbi_ref[g], ohm_ref[...])
        sw_hi, sw_lo = _split_bf16(jnp.concatenate([prt * bbr - pit * bbi, prt * bbi + pit * bbr], axis=0))
        c_hi, c_lo = _split_bf16(cc_ref[g])
        for l in range(nl):
            cw_ref[l * mm:(l + 1) * mm, :] = (cc_ref[g] * prx_ref[g, l:l + 1, :]
                                              + cc2_ref[g] * pix_ref[g, l:l + 1, :]).astype(cw_ref.dtype)
        kcat = (jnp.dot(c_hi, sw_hi, preferred_element_type=F32)
                + jnp.dot(c_hi, sw_lo, preferred_element_type=F32)
                + jnp.dot(c_lo, sw_hi, preferred_element_type=F32))
        z = jnp.concatenate([kcat, jnp.zeros_like(kcat)], axis=1)
        rolled = [z if r == 0 else pltpu.roll(z, 2 * width - r, 1) for r in range(0, LANES, mm)]
        for l in range(nl):
            off = (nl - 1 - l) * mm
            base = off - off % LANES
            strip = rolled[(off % LANES) // mm][:, base:base + width]
            toep_ref[l * mm:(l + 1) * mm, :] = strip.astype(toep_ref.dtype)
        s = jnp.dot(sw_hi, u, preferred_element_type=F32)
        xr, xi = s[:p], s[p:]
        lr, li = lr_ref[g], li_ref[g]
        k = 1
        while k < chunks_per_seq:
            sr, si = shifted(xr, k), shifted(xi, k)
            lrw, liw = jnp.concatenate([lr] * reps, axis=1), jnp.concatenate([li] * reps, axis=1)
            xr, xi = xr + (lrw * sr - liw * si), xi + (lrw * si + liw * sr)
            lr, li = lr * lr - li * li, 2.0 * (lr * li)
            k *= 2
        h = jnp.concatenate([shifted(xr, 1), shifted(xi, 1)], axis=0).astype(BF16)
        yt_ref[g] = (jnp.dot(toep_ref[...], u, preferred_element_type=F32)
                     + jnp.dot(cw_ref[...], h, preferred_element_type=F32))

    for b in range(ncol // nc):
        for l in range(nl):
            rows = slice((b * nl + l) * nc, (b * nl + l + 1) * nc)
            zt = jnp.concatenate([yt_ref[g, l * mm:(l + 1) * mm, b * nc:(b + 1) * nc] for g in range(gb)], axis=0)
            y = zt.T + d_ref[...] * u_ref[rows, :].astype(F32)
            y_ref[rows, :] = _gelu_tanh(y).astype(y_ref.dtype)


def _s5_tables(a_re, a_im, log_dt, b_re, b_im, c_re, c_im, chunk):
    g, p, m = b_re.shape
    dt = jnp.exp(log_dt.astype(F32))[:, None]
    a_re, a_im = a_re.astype(F32), a_im.astype(F32)
    mag = jnp.exp(a_re * dt)
    lb_re, lb_im = mag * jnp.cos(a_im * dt), mag * jnp.sin(a_im * dt)
    den = a_re * a_re + a_im * a_im
    nr, ni = lb_re - 1.0, lb_im
    fr = (nr * a_re + ni * a_im) / den
    fi = (ni * a_re - nr * a_im) / den
    bb_re = fr[..., None] * b_re - fi[..., None] * b_im
    bb_im = fr[..., None] * b_im + fi[..., None] * b_re
    d = jnp.arange(chunk + 1, dtype=F32)[None, :, None]
    pmag = jnp.exp((a_re * dt)[:, None, :] * d)
    pang = (a_im * dt)[:, None, :] * d
    pr, pi = pmag * jnp.cos(pang), pmag * jnp.sin(pang)
    c_re, c_im = c_re.astype(F32), c_im.astype(F32)
    pad = lambda a: jnp.pad(a, ((0, 0), (0, 0), (0, LANES - a.shape[2])))
    rev = lambda a: pad(jnp.flip(a[:, :chunk], axis=1).transpose(0, 2, 1))
    onehot_l = jnp.repeat(jnp.eye(LANES, chunk, dtype=BF16), m, axis=1)
    onehot_m = jnp.tile(jnp.eye(LANES, m, dtype=BF16), (1, chunk))
    cc = jnp.concatenate([c_re, -c_im], axis=-1)
    cc2 = jnp.concatenate([-c_im, -c_re], axis=-1)
    prx = jnp.concatenate([pr[:, 1:], pr[:, 1:]], axis=-1)
    pix = jnp.concatenate([pi[:, 1:], pi[:, 1:]], axis=-1)
    lam_r = jnp.broadcast_to(pr[:, chunk, :, None], (g, p, LANES))
    lam_i = jnp.broadcast_to(pi[:, chunk, :, None], (g, p, LANES))
    return (rev(pr), rev(pi), pad(bb_re), pad(bb_im), cc, cc2, prx, pix, lam_r, lam_i), (onehot_l, onehot_m)


def _s5(proj, d_skip, tables, *, batch, seq):
    per_group, onehots = tables
    cc, prx = per_group[4], per_group[6]
    t = proj.shape[0]
    g, m = cc.shape[0], cc.shape[1]
    e = g * m
    chunk = prx.shape[1]
    nc = seq // chunk
    ncol = t // chunk
    gb = LANES // m
    up = proj.reshape(batch, nc, chunk, proj.shape[1])[..., :e].transpose(0, 2, 1, 3).reshape(t, e)
    wspec = lambda a: pl.BlockSpec((gb,) + a.shape[1:], lambda i: (i, 0, 0))
    const = lambda a: pl.BlockSpec(a.shape, lambda i: (0, 0))
    yp = pl.pallas_call(
        functools.partial(_s5_kernel, chunks_per_seq=nc),
        grid=(g // gb,),
        in_specs=([pl.BlockSpec((t, LANES), lambda i: (0, i)), pl.BlockSpec((1, LANES), lambda i: (0, i))]
                  + [wspec(a) for a in per_group] + [const(a) for a in onehots]),
        out_specs=pl.BlockSpec((t, LANES), lambda i: (0, i)),
        out_shape=jax.ShapeDtypeStruct((t, e), BF16),
        scratch_shapes=[pltpu.VMEM((gb, chunk * m, ncol), BF16), pltpu.VMEM((gb, chunk * m, ncol), F32),
                        pltpu.VMEM((chunk * m, chunk * m), BF16), pltpu.VMEM((chunk * m, 2 * S5_STATE), BF16)],
        compiler_params=_params("parallel"),
        name="s5_chunked",
    )(up, d_skip.astype(F32).reshape(1, e), *per_group, *onehots)
    return yp.reshape(batch, chunk, nc, e).transpose(0, 2, 1, 3).reshape(t, e)


def _glu_out_kernel(z_ref, gate_ref, wg_ref, b_ref, wo_ref, x_ref, o_ref):
    half = o_ref.shape[0] // 2
    for r0 in (0, half):
        rs = slice(r0, r0 + half)
        z = z_ref[rs, :]
        t = jnp.dot(z, wg_ref[...], preferred_element_type=F32) + b_ref[...]
        out = z.astype(F32) * jax.nn.sigmoid(t) * _silu(gate_ref[rs, :].astype(F32))
        o_ref[rs, :] = x_ref[rs, :] + jnp.dot(out.astype(BF16), wo_ref[...], preferred_element_type=F32)


def _glu_out(z, proj, w_glu, b_glu, w_out, x, *, tm):
    t, e = z.shape
    n = w_out.shape[1]
    resident = lambda a: pl.BlockSpec(a.shape, lambda i: (0, 0), pipeline_mode=pl.Buffered(1))
    return pl.pallas_call(
        _glu_out_kernel,
        grid=(t // tm,),
        in_specs=[
            pl.BlockSpec((tm, e), lambda i: (i, 0)),
            pl.BlockSpec((tm, e), lambda i: (i, 1)),
            resident(w_glu),
            pl.BlockSpec((1, e), lambda i: (0, 0)),
            resident(w_out),
            pl.BlockSpec((tm, n), lambda i: (i, 0)),
        ],
        out_specs=pl.BlockSpec((tm, n), lambda i: (i, 0)),
        out_shape=jax.ShapeDtypeStruct((t, n), F32),
        compiler_params=_params("parallel"),
        name="glu_out_residual",
    )(z, proj, w_glu, b_glu.astype(F32).reshape(1, e), w_out, x)


def _even_layer(x, layer_idx, norm_g, w_in, conv_w, conv_b, cln_g, cln_b, qn_g, kn_g,
                lam_q1, lam_k1, lam_q2, lam_k2, subln_g, w_out, *, batch, seq):
    conv_ch = conv_w.shape[1]
    proj = _rms_matmul(x, norm_g, w_in.astype(BF16), tm=512, nsplit=2)
    mix_a = _conv_mixer(proj, conv_w.astype(F32), conv_b.astype(F32), cln_g.astype(F32),
                        cln_b.astype(F32), seq=seq, tt=512)
    qn, kn = _qkv_prep(proj, qn_g, kn_g, seq=seq, tt=512, col0=3 * conv_ch)
    lam_init = 0.8 - 0.6 * math.exp(-0.3 * layer_idx)
    lam = (jnp.exp(jnp.sum(lam_q1.astype(F32) * lam_k1.astype(F32)))
           - jnp.exp(jnp.sum(lam_q2.astype(F32) * lam_k2.astype(F32))) + lam_init)
    qkv_w = qn.shape[1]
    score_bound = (DA_HEAD_DIM * ATT_Q_SCALE * 1.02 * jnp.max(jnp.abs(qn_g.astype(F32)))
                   * jnp.max(jnp.abs(kn_g.astype(F32))) + 0.5)
    mix_b = _diff_attention(lam, score_bound, qn, kn, proj, subln_g, batch=batch, seq=seq, tq=512,
                            v_col0=3 * conv_ch + 2 * qkv_w, gate_col0=3 * conv_ch + 3 * qkv_w,
                            post_scale=1.0 - lam_init)
    return _proj_residual(mix_a, mix_b, w_out.astype(BF16), x, tm=512)


def _odd_layer(x, norm_g, w_in, a_re, a_im, log_dt, b_re, b_im, c_re, c_im, d_skip,
               w_glu, b_glu, w_out, *, batch, seq):
    proj = _rms_matmul(x, norm_g, w_in.astype(BF16), tm=512, nsplit=1)
    tables = _s5_tables(a_re, a_im, log_dt, b_re, b_im, c_re, c_im, S5_CHUNK)
    z = _s5(proj, d_skip, tables, batch=batch, seq=seq)
    return _glu_out(z, proj, w_glu.astype(BF16), b_glu, w_out.astype(BF16), x, tm=512)


def kernel(x, e_norm_g, e_w_in, e_conv_w, e_conv_b, e_cln_g, e_cln_b, e_qn_g, e_kn_g, e_lam_q1, e_lam_k1, e_lam_q2, e_lam_k2, e_subln_g, e_w_out, o_norm_g, o_w_in, o_A_re, o_A_im, o_log_dt, o_B_re, o_B_im, o_C_re, o_C_im, o_D, o_w_glu, o_b_glu, o_w_out):
    batch, seq, d_model = x.shape
    depth = e_norm_g.shape[0] + o_norm_g.shape[0]
    h = x.reshape(batch * seq, d_model)
    for layer in range(depth):
        j = layer // 2
        if layer % 2 == 0:
            h = _even_layer(h, layer, e_norm_g[j], e_w_in[j], e_conv_w[j], e_conv_b[j], e_cln_g[j],
                            e_cln_b[j], e_qn_g[j], e_kn_g[j], e_lam_q1[j], e_lam_k1[j], e_lam_q2[j],
                            e_lam_k2[j], e_subln_g[j], e_w_out[j], batch=batch, seq=seq)
        else:
            h = _odd_layer(h, o_norm_g[j], o_w_in[j], o_A_re[j], o_A_im[j], o_log_dt[j], o_B_re[j],
                           o_B_im[j], o_C_re[j], o_C_im[j], o_D[j], o_w_glu[j], o_b_glu[j], o_w_out[j],
                           batch=batch, seq=seq)
    return h.reshape(batch, seq, d_model)
```

```python
import functools
import math

import jax
import jax.numpy as jnp
from jax import lax
from jax.experimental import pallas as pl
from jax.experimental.pallas import tpu as pltpu

F32 = jnp.float32
BF16 = jnp.bfloat16

EPS = 1e-6
ROPE_THETA = 10000.0
CONV_W = 31
DA_HEADS = 8
DA_HEAD_DIM = 64
S5_GROUP = 16
S5_STATE = 64

LANES = 128
CONV_HALO = 32
CONV_ROWS = 32
ATT_ROWS = 64
ATT_FIXED_SHIFT_MAX = 40.0
ATT_Q_SCALE = math.log2(math.e) * DA_HEAD_DIM ** -0.5
S5_CHUNK = 32
VMEM_LIMIT = 56 * 1024 * 1024


def _params(*sem):
    return pltpu.CompilerParams(dimension_semantics=sem, vmem_limit_bytes=VMEM_LIMIT)


def _silu(x):
    return x * jax.nn.sigmoid(x)


def _gelu_tanh(x):
    return 0.5 * x * (1.0 + jnp.tanh(math.sqrt(2.0 / math.pi) * (x + 0.044715 * (x * x * x))))


def _rms_matmul_kernel(x_ref, g_ref, w_ref, o_ref):
    half = o_ref.shape[0] // 2
    for r0 in (0, half):
        rs = slice(r0, r0 + half)
        x = x_ref[rs, :]
        ms = jnp.mean(x * x, axis=-1, keepdims=True)
        h = (x * lax.rsqrt(ms + EPS) * g_ref[...]).astype(BF16)
        o_ref[rs, :] = jnp.dot(h, w_ref[...], preferred_element_type=F32).astype(o_ref.dtype)


def _rms_matmul(x, g, w, *, tm, nsplit):
    m, d = x.shape
    n = w.shape[1]
    tn = n // nsplit
    return pl.pallas_call(
        _rms_matmul_kernel,
        grid=(nsplit, m // tm),
        in_specs=[
            pl.BlockSpec((tm, d), lambda j, i: (i, 0)),
            pl.BlockSpec((1, d), lambda j, i: (0, 0)),
            pl.BlockSpec((d, tn), lambda j, i: (0, j), pipeline_mode=pl.Buffered(1)),
        ],
        out_specs=pl.BlockSpec((tm, tn), lambda j, i: (i, j)),
        out_shape=jax.ShapeDtypeStruct((m, n), BF16),
        compiler_params=_params("parallel", "parallel"),
        name="rms_matmul",
    )(x, g.reshape(1, d), w)


def _conv_kernel(val_ref, glu_ref, gate_ref, hval_ref, hglu_ref, w_ref, cb_ref, lg_ref, lb_ref,
                 o_ref, uext_ref, *, tt, tiles_per_seq):
    c = o_ref.shape[-1]
    nlt = c // LANES
    first = (pl.program_id(0) % tiles_per_seq) == 0
    f32 = lambda ref, idx=Ellipsis: ref[idx].astype(F32)
    halo = jnp.where(first, 0.0, f32(hval_ref) * jax.nn.sigmoid(f32(hglu_ref)))
    cur = f32(val_ref) * jax.nn.sigmoid(f32(glu_ref))
    for j in range(nlt):
        sl = slice(j * LANES, (j + 1) * LANES)
        uext_ref[j, pl.ds(0, CONV_HALO, stride=2), :] = halo[:, sl]
        uext_ref[j, pl.ds(2 * CONV_HALO, tt, stride=2), :] = cur[:, sl]
    lg = lg_ref[...]
    lb = lb_ref[...]

    def chunk(r, carry):
        base = pl.multiple_of(r * CONV_ROWS, CONV_ROWS)
        accs = []
        for j in range(nlt):
            sl = slice(j * LANES, (j + 1) * LANES)
            acc = jnp.broadcast_to(cb_ref[:, sl], (CONV_ROWS, LANES))
            for k in range(CONV_W):
                row = base + (CONV_HALO - CONV_W + 1 + k)
                acc = acc + w_ref[k:k + 1, sl] * uext_ref[j, pl.ds(2 * row, CONV_ROWS, stride=2), :]
            accs.append(acc)
        acc = jnp.concatenate(accs, axis=1)
        mean = jnp.mean(acc, axis=-1, keepdims=True)
        xc = acc - mean
        var = jnp.mean(xc * xc, axis=-1, keepdims=True)
        y = _silu(xc * lax.rsqrt(var + EPS) * lg + lb)
        gate = gate_ref[pl.ds(base, CONV_ROWS), :].astype(F32)
        o_ref[pl.ds(base, CONV_ROWS), :] = (y * _silu(gate)).astype(o_ref.dtype)
        return carry

    lax.fori_loop(0, tt // CONV_ROWS, chunk, 0, unroll=2)


def _conv_mixer(proj, conv_w, conv_b, ln_g, ln_b, *, seq, tt):
    t = proj.shape[0]
    c = conv_w.shape[1]
    hb = tt // CONV_HALO
    row = lambda a: a.reshape(1, c)
    return pl.pallas_call(
        functools.partial(_conv_kernel, tt=tt, tiles_per_seq=seq // tt),
        grid=(t // tt,),
        in_specs=[
            pl.BlockSpec((tt, c), lambda i: (i, 0)),
            pl.BlockSpec((tt, c), lambda i: (i, 1)),
            pl.BlockSpec((tt, c), lambda i: (i, 2)),
            pl.BlockSpec((CONV_HALO, c), lambda i: (jnp.maximum(i * hb - 1, 0), 0)),
            pl.BlockSpec((CONV_HALO, c), lambda i: (jnp.maximum(i * hb - 1, 0), 1)),
            pl.BlockSpec((CONV_W, c), lambda i: (0, 0)),
            pl.BlockSpec((1, c), lambda i: (0, 0)),
            pl.BlockSpec((1, c), lambda i: (0, 0)),
            pl.BlockSpec((1, c), lambda i: (0, 0)),
        ],
        out_specs=pl.BlockSpec((tt, c), lambda i: (i, 0)),
        out_shape=jax.ShapeDtypeStruct((t, c), BF16),
        scratch_shapes=[pltpu.VMEM((c // LANES, 2 * (CONV_HALO + tt), LANES), F32)],
        compiler_params=_params("parallel"),
        name="conv_mixer",
    )(proj, proj, proj, proj, proj, conv_w, row(conv_b), row(ln_g), row(ln_b))


def _qkv_kernel(q_ref, k_ref, cos_ref, sin_ref, qg_ref, kg_ref, ones_ref, perm_ref, qo_ref, ko_ref, *, q_scale):
    cos = cos_ref[...]
    sin = sin_ref[...]

    def hi_lo_dot(v, rhs2):
        hi, lo = _split_bf16(v)
        return jnp.dot(jnp.concatenate([hi, lo], axis=1), rhs2, preferred_element_type=F32)

    def prep(x, g, scale):
        ss = hi_lo_dot(x * x, ones_ref[...])
        y = x * lax.rsqrt(ss * (1.0 / DA_HEAD_DIM) + EPS) * g
        partner = hi_lo_dot(y, perm_ref[...])
        return (y * cos + partner * sin) * scale

    for h in range(q_ref.shape[-1] // LANES):
        sl = slice(h * LANES, (h + 1) * LANES)
        qo_ref[:, sl] = prep(q_ref[:, sl].astype(F32), qg_ref[...], q_scale).astype(qo_ref.dtype)
        ko_ref[:, sl] = prep(k_ref[:, sl].astype(F32), kg_ref[...], 1.0).astype(ko_ref.dtype)


def _qkv_prep(proj, qn_g, kn_g, *, seq, tt, col0):
    t = proj.shape[0]
    w = DA_HEADS * 2 * DA_HEAD_DIM
    half = DA_HEAD_DIM // 2
    freqs = ROPE_THETA ** (-jnp.arange(half, dtype=F32) / half)
    ang = jnp.arange(seq, dtype=F32)[:, None] * freqs[None, :]
    cos = jnp.tile(jnp.cos(ang), (1, LANES // half))
    sin = jnp.tile(jnp.concatenate([-jnp.sin(ang), jnp.sin(ang)], axis=-1), (1, LANES // DA_HEAD_DIM))
    blk = jnp.arange(LANES) // DA_HEAD_DIM
    ones = (blk[:, None] == blk[None, :]).astype(BF16)
    src = jnp.arange(LANES)
    src = jnp.where(src % DA_HEAD_DIM < half, src + half, src - half)
    perm = (jnp.arange(LANES)[:, None] == src[None, :]).astype(BF16)
    ones, perm = jnp.concatenate([ones, ones], axis=0), jnp.concatenate([perm, perm], axis=0)
    tile_g = lambda g: jnp.tile(g.astype(F32), LANES // DA_HEAD_DIM).reshape(1, LANES)
    cb = col0 // w
    nseq = seq // tt
    out = jax.ShapeDtypeStruct((t, w), BF16)
    return pl.pallas_call(
        functools.partial(_qkv_kernel, q_scale=ATT_Q_SCALE),
        grid=(t // tt,),
        in_specs=[
            pl.BlockSpec((tt, w), lambda i: (i, cb)),
            pl.BlockSpec((tt, w), lambda i: (i, cb + 1)),
            pl.BlockSpec((tt, LANES), lambda i: (i % nseq, 0)),
            pl.BlockSpec((tt, LANES), lambda i: (i % nseq, 0)),
            pl.BlockSpec((1, LANES), lambda i: (0, 0)),
            pl.BlockSpec((1, LANES), lambda i: (0, 0)),
            pl.BlockSpec((2 * LANES, LANES), lambda i: (0, 0)),
            pl.BlockSpec((2 * LANES, LANES), lambda i: (0, 0)),
        ],
        out_specs=[pl.BlockSpec((tt, w), lambda i: (i, 0))] * 2,
        out_shape=[out, out],
        compiler_params=_params("parallel"),
        name="qkv_prep",
    )(proj, proj, cos, sin, tile_g(qn_g), tile_g(kn_g), ones, perm)


def _attn_kernel(lam_ref, q_ref, k_ref, v_ref, bg_ref, sg_ref, o_ref,
                 qs_ref, vaug_ref, s0_ref, s1_ref, s2_ref, p0_ref, p1_ref, p2_ref, m_ref,
                 a0_ref, a1_ref, a2_ref, acc_ref, *, tq, post_scale):
    i = pl.program_id(2)
    s_refs, p_refs, a_refs = (s0_ref, s1_ref, s2_ref), (p0_ref, p1_ref, p2_ref), (a0_ref, a1_ref, a2_ref)

    @pl.when(i == 0)
    def _():
        vaug_ref[:, 0:LANES] = v_ref[...]
        vaug_ref[:, LANES:2 * LANES] = jnp.ones(v_ref.shape, vaug_ref.dtype)

    q = q_ref[...]
    lane = lax.broadcasted_iota(jnp.int32, q.shape, 1)
    zero = jnp.zeros_like(q)
    qs_ref[0:tq, :] = jnp.where(lane < DA_HEAD_DIM, q, zero)
    qs_ref[tq:2 * tq, :] = jnp.where(lane >= DA_HEAD_DIM, q, zero)
    acc_ref[...] = jnp.zeros(acc_ref.shape, F32)
    reps = tq // LANES
    bound = lam_ref[1]
    fixed_shift = bound <= ATT_FIXED_SHIFT_MAX

    def scores(j, slot):
        start = pl.multiple_of(j * tq, tq)
        s_refs[slot][...] = lax.dot_general(qs_ref[...], k_ref[pl.ds(start, tq), :],
                                            (((1,), (1,)), ((), ())), preferred_element_type=F32)

    def masked_scores(s_ref, rs, r0, masked):
        s = s_ref[rs, :]
        if masked:
            row = (r0 % tq) + lax.broadcasted_iota(jnp.int32, s.shape, 0)
            col = lax.broadcasted_iota(jnp.int32, s.shape, 1)
            s = jnp.where(col <= row, s, -jnp.inf)
        return s

    def softmax_online(slot, masked):
        s_ref, p_ref, a_ref = s_refs[slot], p_refs[slot], a_refs[slot]
        for r0 in range(0, 2 * tq, ATT_ROWS):
            rs = slice(r0, r0 + ATT_ROWS)
            s = masked_scores(s_ref, rs, r0, masked)
            m_prev = m_ref[rs, :]
            m_new = jnp.maximum(m_prev, jnp.max(s, axis=-1, keepdims=True))
            a_ref[rs, :] = jnp.exp2(m_prev - m_new)
            m_ref[rs, :] = m_new
            p_ref[rs, :] = jnp.exp2(s - jnp.concatenate([m_new] * reps, axis=1)).astype(p_ref.dtype)

    def accumulate_online(j, slot):
        start = pl.multiple_of(j * tq, tq)
        alpha = a_refs[slot][...]
        acc_ref[...] = jnp.concatenate([alpha, alpha], axis=1) * acc_ref[...] + jnp.dot(
            p_refs[slot][...], vaug_ref[pl.ds(start, tq), :], preferred_element_type=F32)

    def softmax_fixed(slot, masked):
        s_ref, p_ref = s_refs[slot], p_refs[slot]
        for r0 in range(0, 2 * tq, ATT_ROWS):
            rs = slice(r0, r0 + ATT_ROWS)
            s = masked_scores(s_ref, rs, r0, masked)
            p_ref[rs, :] = jnp.exp2(s - bound).astype(p_ref.dtype)

    def accumulate_fixed(j, slot):
        start = pl.multiple_of(j * tq, tq)
        acc_ref[...] = acc_ref[...] + jnp.dot(p_refs[slot][...], vaug_ref[pl.ds(start, tq), :],
                                              preferred_element_type=F32)

    def run(softmax, accumulate):
        scores(0, 0)

        @pl.when(i == 0)
        def _():
            softmax(0, True)
            accumulate(0, 0)

        @pl.when(i > 0)
        def _():
            scores(1, 1)
            softmax(0, False)
            npairs = (i - 1) // 2

            def pair(tt, carry):
                t = 2 * tt
                scores(t + 2, 0)
                softmax(1, False)
                accumulate(t, 0)
                scores(t + 3, 1)
                softmax(0, False)
                accumulate(t + 1, 1)
                return carry

            lax.fori_loop(0, npairs, pair, 0)

            @pl.when(i % 2 == 1)
            def _():
                softmax(1, True)
                accumulate(i - 1, 0)
                accumulate(i, 1)

            @pl.when(i % 2 == 0)
            def _():
                scores(i, 2)
                softmax(1, False)
                accumulate(i - 2, 0)
                softmax(2, True)
                accumulate(i - 1, 1)
                accumulate(i, 2)

    @pl.when(fixed_shift)
    def _():
        run(softmax_fixed, accumulate_fixed)

    @pl.when(jnp.logical_not(fixed_shift))
    def _():
        m_ref[...] = jnp.full(m_ref.shape, -jnp.inf, F32)
        run(softmax_online, accumulate_online)

    o = acc_ref[:, 0:LANES] / acc_ref[:, LANES:2 * LANES]
    d = o[:tq] - lam_ref[0] * o[tq:]
    ms = jnp.mean(d * d, axis=-1, keepdims=True)
    y = d * lax.rsqrt(ms + EPS) * sg_ref[...] * post_scale
    o_ref[...] = (y * _silu(bg_ref[...].astype(F32))).astype(o_ref.dtype)


def _diff_attention(lam, score_bound, qn, kn, proj, subln_g, *, batch, seq, tq, v_col0, gate_col0, post_scale):
    t, w = qn.shape
    nq = seq // tq
    vcb = v_col0 // LANES
    gcb = gate_col0 // LANES
    stat = pltpu.VMEM((2 * tq, LANES), F32)
    sbuf = pltpu.VMEM((2 * tq, tq), F32)
    pbuf = pltpu.VMEM((2 * tq, tq), BF16)
    return pl.pallas_call(
        functools.partial(_attn_kernel, tq=tq, post_scale=post_scale),
        grid=(batch, DA_HEADS, nq),
        in_specs=[
            pl.BlockSpec(memory_space=pltpu.SMEM),
            pl.BlockSpec((tq, LANES), lambda b, h, i: (b * nq + i, h)),
            pl.BlockSpec((seq, LANES), lambda b, h, i: (b, h)),
            pl.BlockSpec((seq, LANES), lambda b, h, i: (b, vcb + h)),
            pl.BlockSpec((tq, LANES), lambda b, h, i: (b * nq + i, gcb + h)),
            pl.BlockSpec((1, LANES), lambda b, h, i: (0, 0)),
        ],
        out_specs=pl.BlockSpec((tq, LANES), lambda b, h, i: (b * nq + i, h)),
        out_shape=jax.ShapeDtypeStruct((t, w), BF16),
        scratch_shapes=[pltpu.VMEM((2 * tq, LANES), BF16), pltpu.VMEM((seq, 2 * LANES), BF16),
                        sbuf, sbuf, sbuf, pbuf, pbuf, pbuf, stat, stat, stat, stat,
                        pltpu.VMEM((2 * tq, 2 * LANES), F32)],
        compiler_params=_params("parallel", "parallel", "arbitrary"),
        name="diff_attention",
    )(jnp.stack([lam, score_bound]).astype(F32), qn, kn, proj, proj, subln_g.astype(F32).reshape(1, LANES))


def _proj_res_kernel(a_ref, b_ref, w_ref, x_ref, o_ref):
    kh = a_ref.shape[1]
    half = o_ref.shape[0] // 2
    for r0 in (0, half):
        rs = slice(r0, r0 + half)
        y = (jnp.dot(a_ref[rs, :], w_ref[0:kh, :], preferred_element_type=F32)
             + jnp.dot(b_ref[rs, :], w_ref[kh:2 * kh, :], preferred_element_type=F32))
        o_ref[rs, :] = x_ref[rs, :] + y


def _proj_residual(a, b, w, x, *, tm):
    m, n = x.shape
    kh = w.shape[0] // 2
    return pl.pallas_call(
        _proj_res_kernel,
        grid=(m // tm,),
        in_specs=[
            pl.BlockSpec((tm, kh), lambda i: (i, 0)),
            pl.BlockSpec((tm, kh), lambda i: (i, 0)),
            pl.BlockSpec(w.shape, lambda i: (0, 0), pipeline_mode=pl.Buffered(1)),
            pl.BlockSpec((tm, n), lambda i: (i, 0)),
        ],
        out_specs=pl.BlockSpec((tm, n), lambda i: (i, 0)),
        out_shape=jax.ShapeDtypeStruct((m, n), F32),
        compiler_params=_params("parallel"),
        name="proj_residual",
    )(a, b, w, x)


def _split_bf16(x):
    hi = x.astype(BF16)
    return hi, (x - hi.astype(F32)).astype(BF16)


def _expand_lanes(x, onehot):
    hi, lo = _split_bf16(x)
    return (jnp.dot(hi, onehot, preferred_element_type=F32) + jnp.dot(lo, onehot, preferred_element_type=F32))


def _s5_kernel(u_ref, d_ref, pvr_ref, pvi_ref, bbr_ref, bbi_ref, cc_ref, cc2_ref, prx_ref, pix_ref,
               lr_ref, li_ref, ohl_ref, ohm_ref, y_ref, ut_ref, yt_ref, toep_ref, cw_ref,
               *, chunks_per_seq):
    p = S5_STATE
    gb, width, ncol = ut_ref.shape
    mm = cc_ref.shape[1]
    nl = width // mm
    nc = chunks_per_seq
    reps = ncol // LANES
    lane = lax.broadcasted_iota(jnp.int32, (p, ncol), 1)
    cpos = lane % nc

    def shifted(x, k):
        return jnp.where(cpos >= k, pltpu.roll(x, k, 1), 0.0)

    for b in range(ncol // nc):
        for l in range(nl):
            xt = u_ref[(b * nl + l) * nc:(b * nl + l + 1) * nc, :].astype(F32).T.astype(ut_ref.dtype)
            for g in range(gb):
                ut_ref[g, l * mm:(l + 1) * mm, b * nc:(b + 1) * nc] = xt[g * mm:(g + 1) * mm, :]

    for g in range(gb):
        u = ut_ref[g]
        prt, pit = _expand_lanes(pvr_ref[g], ohl_ref[...]), _expand_lanes(pvi_ref[g], ohl_ref[...])
        bbr, bbi = _expand_lanes(bbr_ref[g], ohm_ref[...]), _expand_lanes(bbi_ref[g], ohm_ref[...])
        sw_hi, sw_lo = _split_bf16(jnp.concatenate([prt * bbr - pit * bbi, prt * bbi + pit * bbr], axis=0))
        c_hi, c_lo = _split_bf16(cc_ref[g])
        for l in range(nl):
            cw_ref[l * mm:(l + 1) * mm, :] = (cc_ref[g] * prx_ref[g, l:l + 1, :]
                                              + cc2_ref[g] * pix_ref[g, l:l + 1, :]).astype(cw_ref.dtype)
        kcat = (jnp.dot(c_hi, sw_hi, preferred_element_type=F32)
                + jnp.dot(c_hi, sw_lo, preferred_element_type=F32)
                + jnp.dot(c_lo, sw_hi, preferred_element_type=F32))
        z = jnp.concatenate([kcat, jnp.zeros_like(kcat)], axis=1)
        rolled = [z if r == 0 else pltpu.roll(z, 2 * width - r, 1) for r in range(0, LANES, mm)]
        for l in range(nl):
            off = (nl - 1 - l) * mm
            base = off - off % LANES
            strip = rolled[(off % LANES) // mm][:, base:base + width]
            toep_ref[l * mm:(l + 1) * mm, :] = strip.astype(toep_ref.dtype)
        s = jnp.dot(sw_hi, u, preferred_element_type=F32)
        xr, xi = s[:p], s[p:]
        lr, li = lr_ref[g], li_ref[g]
        k = 1
        while k < chunks_per_seq:
            sr, si = shifted(xr, k), shifted(xi, k)
            lrw, liw = jnp.concatenate([lr] * reps, axis=1), jnp.concatenate([li] * reps, axis=1)
            xr, xi = xr + (lrw * sr - liw * si), xi + (lrw * si + liw * sr)
            lr, li = lr * lr - li * li, 2.0 * (lr * li)
            k *= 2
        h = jnp.concatenate([shifted(xr, 1), shifted(xi, 1)], axis=0).astype(BF16)
        yt_ref[g] = (jnp.dot(toep_ref[...], u, preferred_element_type=F32)
                     + jnp.dot(cw_ref[...], h, preferred_element_type=F32))

    for b in range(ncol // nc):
        for l in range(nl):
            rows = slice((b * nl + l) * nc, (b * nl + l + 1) * nc)
            zt = jnp.concatenate([yt_ref[g, l * mm:(l + 1) * mm, b * nc:(b + 1) * nc] for g in range(gb)], axis=0)
            y = zt.T + d_ref[...] * u_ref[rows, :].astype(F32)
            y_ref[rows, :] = _gelu_tanh(y).astype(y_ref.dtype)


def _s5_tables(a_re, a_im, log_dt, b_re, b_im, c_re, c_im, chunk):
    g, p, m = b_re.shape
    dt = jnp.exp(log_dt.astype(F32))[:, None]
    a_re, a_im = a_re.astype(F32), a_im.astype(F32)
    mag = jnp.exp(a_re * dt)
    lb_re, lb_im = mag * jnp.cos(a_im * dt), mag * jnp.sin(a_im * dt)
    den = a_re * a_re + a_im * a_im
    nr, ni = lb_re - 1.0, lb_im
    fr = (nr * a_re + ni * a_im) / den
    fi = (ni * a_re - nr * a_im) / den
    bb_re = fr[..., None] * b_re - fi[..., None] * b_im
    bb_im = fr[..., None] * b_im + fi[..., None] * b_re
    d = jnp.arange(chunk + 1, dtype=F32)[None, :, None]
    pmag = jnp.exp((a_re * dt)[:, None, :] * d)
    pang = (a_im * dt)[:, None, :] * d
    pr, pi = pmag * jnp.cos(pang), pmag * jnp.sin(pang)
    c_re, c_im = c_re.astype(F32), c_im.astype(F32)
    pad = lambda a: jnp.pad(a, ((0, 0), (0, 0), (0, LANES - a.shape[2])))
    rev = lambda a: pad(jnp.flip(a[:, :chunk], axis=1).transpose(0, 2, 1))
    onehot_l = jnp.repeat(jnp.eye(LANES, chunk, dtype=BF16), m, axis=1)
    onehot_m = jnp.tile(jnp.eye(LANES, m, dtype=BF16), (1, chunk))
    cc = jnp.concatenate([c_re, -c_im], axis=-1)
    cc2 = jnp.concatenate([-c_im, -c_re], axis=-1)
    prx = jnp.concatenate([pr[:, 1:], pr[:, 1:]], axis=-1)
    pix = jnp.concatenate([pi[:, 1:], pi[:, 1:]], axis=-1)
    lam_r = jnp.broadcast_to(pr[:, chunk, :, None], (g, p, LANES))
    lam_i = jnp.broadcast_to(pi[:, chunk, :, None], (g, p, LANES))
    return (rev(pr), rev(pi), pad(bb_re), pad(bb_im), cc, cc2, prx, pix, lam_r, lam_i), (onehot_l, onehot_m)


def _s5(proj, d_skip, tables, *, batch, seq):
    per_group, onehots = tables
    cc, prx = per_group[4], per_group[6]
    t = proj.shape[0]
    g, m = cc.shape[0], cc.shape[1]
    e = g * m
    chunk = prx.shape[1]
    nc = seq // chunk
    ncol = t // chunk
    gb = LANES // m
    up = proj.reshape(batch, nc, chunk, proj.shape[1])[..., :e].transpose(0, 2, 1, 3).reshape(t, e)
    wspec = lambda a: pl.BlockSpec((gb,) + a.shape[1:], lambda i: (i, 0, 0))
    const = lambda a: pl.BlockSpec(a.shape, lambda i: (0, 0))
    yp = pl.pallas_call(
        functools.partial(_s5_kernel, chunks_per_seq=nc),
        grid=(g // gb,),
        in_specs=([pl.BlockSpec((t, LANES), lambda i: (0, i)), pl.BlockSpec((1, LANES), lambda i: (0, i))]
                  + [wspec(a) for a in per_group] + [const(a) for a in onehots]),
        out_specs=pl.BlockSpec((t, LANES), lambda i: (0, i)),
        out_shape=jax.ShapeDtypeStruct((t, e), BF16),
        scratch_shapes=[pltpu.VMEM((gb, chunk * m, ncol), BF16), pltpu.VMEM((gb, chunk * m, ncol), F32),
                        pltpu.VMEM((chunk * m, chunk * m), BF16), pltpu.VMEM((chunk * m, 2 * S5_STATE), BF16)],
        compiler_params=_params("parallel"),
        name="s5_chunked",
    )(up, d_skip.astype(F32).reshape(1, e), *per_group, *onehots)
    return yp.reshape(batch, chunk, nc, e).transpose(0, 2, 1, 3).reshape(t, e)


def _glu_out_kernel(z_ref, gate_ref, wg_ref, b_ref, wo_ref, x_ref, o_ref):
    half = o_ref.shape[0] // 2
    for r0 in (0, half):
        rs = slice(r0, r0 + half)
        z = z_ref[rs, :]
        t = jnp.dot(z, wg_ref[...], preferred_element_type=F32) + b_ref[...]
        out = z.astype(F32) * jax.nn.sigmoid(t) * _silu(gate_ref[rs, :].astype(F32))
        o_ref[rs, :] = x_ref[rs, :] + jnp.dot(out.astype(BF16), wo_ref[...], preferred_element_type=F32)


def _glu_out(z, proj, w_glu, b_glu, w_out, x, *, tm):
    t, e = z.shape
    n = w_out.shape[1]
    resident = lambda a: pl.BlockSpec(a.shape, lambda i: (0, 0), pipeline_mode=pl.Buffered(1))
    return pl.pallas_call(
        _glu_out_kernel,
        grid=(t // tm,),
        in_specs=[
            pl.BlockSpec((tm, e), lambda i: (i, 0)),
            pl.BlockSpec((tm, e), lambda i: (i, 1)),
            resident(w_glu),
            pl.BlockSpec((1, e), lambda i: (0, 0)),
            resident(w_out),
            pl.BlockSpec((tm, n), lambda i: (i, 0)),
        ],
        out_specs=pl.BlockSpec((tm, n), lambda i: (i, 0)),
        out_shape=jax.ShapeDtypeStruct((t, n), F32),
        compiler_params=_params("parallel"),
        name="glu_out_residual",
    )(z, proj, w_glu, b_glu.astype(F32).reshape(1, e), w_out, x)


def _even_layer(x, layer_idx, norm_g, w_in, conv_w, conv_b, cln_g, cln_b, qn_g, kn_g,
                lam_q1, lam_k1, lam_q2, lam_k2, subln_g, w_out, *, batch, seq):
    conv_ch = conv_w.shape[1]
    proj = _rms_matmul(x, norm_g, w_in.astype(BF16), tm=512, nsplit=2)
    mix_a = _conv_mixer(proj, conv_w.astype(F32), conv_b.astype(F32), cln_g.astype(F32),
                        cln_b.astype(F32), seq=seq, tt=512)
    qn, kn = _qkv_prep(proj, qn_g, kn_g, seq=seq, tt=512, col0=3 * conv_ch)
    lam_init = 0.8 - 0.6 * math.exp(-0.3 * layer_idx)
    lam = (jnp.exp(jnp.sum(lam_q1.astype(F32) * lam_k1.astype(F32)))
           - jnp.exp(jnp.sum(lam_q2.astype(F32) * lam_k2.astype(F32))) + lam_init)
    qkv_w = qn.shape[1]
    score_bound = (DA_HEAD_DIM * ATT_Q_SCALE * 1.02 * jnp.max(jnp.abs(qn_g.astype(F32)))
                   * jnp.max(jnp.abs(kn_g.astype(F32))) + 0.5)
    mix_b = _diff_attention(lam, score_bound, qn, kn, proj, subln_g, batch=batch, seq=seq, tq=512,
                            v_col0=3 * conv_ch + 2 * qkv_w, gate_col0=3 * conv_ch + 3 * qkv_w,
                            post_scale=1.0 - lam_init)
    return _proj_residual(mix_a, mix_b, w_out.astype(BF16), x, tm=512)


def _odd_layer(x, norm_g, w_in, a_re, a_im, log_dt, b_re, b_im, c_re, c_im, d_skip,
               w_glu, b_glu, w_out, *, batch, seq):
    proj = _rms_matmul(x, norm_g, w_in.astype(BF16), tm=512, nsplit=1)
    tables = _s5_tables(a_re, a_im, log_dt, b_re, b_im, c_re, c_im, S5_CHUNK)
    z = _s5(proj, d_skip, tables, batch=batch, seq=seq)
    return _glu_out(z, proj, w_glu.astype(BF16), b_glu, w_out.astype(BF16), x, tm=512)


def kernel(x, e_norm_g, e_w_in, e_conv_w, e_conv_b, e_cln_g, e_cln_b, e_qn_g, e_kn_g, e_lam_q1, e_lam_k1, e_lam_q2, e_lam_k2, e_subln_g, e_w_out, o_norm_g, o_w_in, o_A_re, o_A_im, o_log_dt, o_B_re, o_B_im, o_C_re, o_C_im, o_D, o_w_glu, o_b_glu, o_w_out):
    batch, seq, d_model = x.shape
    depth = e_norm_g.shape[0] + o_norm_g.shape[0]
    h = x.reshape(batch * seq, d_model)
    for layer in range(depth):
        j = layer // 2
        if layer % 2 == 0:
            h = _even_layer(h, layer, e_norm_g[j], e_w_in[j], e_conv_w[j], e_conv_b[j], e_cln_g[j],
                            e_cln_b[j], e_qn_g[j], e_kn_g[j], e_lam_q1[j], e_lam_k1[j], e_lam_q2[j],
                            e_lam_k2[j], e_subln_g[j], e_w_out[j], batch=batch, seq=seq)
        else:
            h = _odd_layer(h, o_norm_g[j], o_w_in[j], o_A_re[j], o_A_im[j], o_log_dt[j], o_B_re[j],
                           o_B_im[j], o_C_re[j], o_C_im[j], o_D[j], o_w_glu[j], o_b_glu[j], o_w_out[j],
                           batch=batch, seq=seq)
    return h.reshape(batch, seq, d_model)
```
